```python
import jax, jax.numpy as jnp
from jax import lax
import numpy as np

D_MODEL = 1024
BATCH = 16
SEQ = 2048
DEPTH = 4

GRID_W = 64
CTX_LEN = 256
N_MIXERS = 2
EXPAND = 2
E_WIDTH = EXPAND * D_MODEL
HEAD_K = 128
N_HEADS = E_WIDTH // HEAD_K
HEAD_V = E_WIDTH // N_HEADS
CHUNK = 16
CONV_W = 31
EPS = 1e-6

kernel_name = "hgrn2_conformer_hybrid_dit"


def _rmsnorm(x, g):
    xf = x.astype(jnp.float32)
    y = xf * lax.rsqrt(jnp.mean(xf * xf, axis=-1, keepdims=True) + EPS)
    return (y * g.astype(jnp.float32)).astype(x.dtype)


def _layernorm(x, g, b):
    xf = x.astype(jnp.float32)
    xc = xf - jnp.mean(xf, axis=-1, keepdims=True)
    var = jnp.mean(xc * xc, axis=-1, keepdims=True)
    return (xc * lax.rsqrt(var + EPS) * g.astype(jnp.float32) + b.astype(jnp.float32)).astype(x.dtype)


def _ada(cvec, w, b):
    m = jax.nn.silu(cvec) @ w + b
    return jnp.split(m, 3, axis=-1)


def _modulate(x, g, shift, scale):
    return _rmsnorm(x, g) * (1 + scale) + shift


def _hgrn_lower_bounds(lb_logits):
    p = jax.nn.softmax(lb_logits.astype(jnp.float32), axis=1)
    return jnp.cumsum(p, axis=1) - p[:, :1]


def _heads(t):
    return t.reshape(t.shape[0], t.shape[1], N_HEADS, -1).astype(jnp.float32)


def _forget(z, lb):
    f = lb + (1 - lb) * jax.nn.sigmoid(z.astype(jnp.float32))
    return _heads(jnp.log(f)), _heads(1 - f)


def _chunk_gla(q, k, v, g, s0):
    b_, length, h_, _ = q.shape
    n = length // CHUNK

    def to_chunks(t):
        return t.reshape(b_, n, CHUNK, h_, t.shape[-1]).transpose(1, 0, 3, 2, 4)

    mask = jnp.tril(jnp.ones((CHUNK, CHUNK), dtype=bool))
    mid = CHUNK // 2

    def step(state, inp):
        qc, kc, vc, gc = inp
        bcum = jnp.cumsum(gc, axis=-2)
        b_ref = bcum[..., mid:mid + 1, :]
        b_end = bcum[..., -1:, :]
        inter = jnp.einsum('bhck,bhkv->bhcv', qc * jnp.exp(bcum), state)
        scores = jnp.einsum('bhtk,bhsk->bhts', qc * jnp.exp(bcum - b_ref), kc * jnp.exp(b_ref - bcum))
        intra = jnp.einsum('bhts,bhsv->bhtv', jnp.where(mask, scores, 0.0), vc)
        new_state = (jnp.exp(b_end[..., 0, :])[..., None] * state
                     + jnp.einsum('bhsk,bhsv->bhkv', kc * jnp.exp(b_end - bcum), vc))
        return new_state, inter + intra

    s_final, o = lax.scan(step, s0, (to_chunks(q), to_chunks(k), to_chunks(v), to_chunks(g)))
    o = o.transpose(1, 0, 3, 2, 4).reshape(b_, length, h_, v.shape[-1])
    return o, s_final


def _chunk_gla_rev(q, k, v, g, s0):
    flip = lambda t: jnp.flip(t, axis=1)
    o, s_final = _chunk_gla(flip(q), flip(k), flip(v), flip(g), s0)
    return flip(o), s_final


def _hgrn_project(h, w_in, lb_f, lb_b):
    p = h @ w_in
    q, v, zf, zb, gate = jnp.split(p, 5, axis=-1)
    gf, kf = _forget(zf, lb_f)
    gb, kb = _forget(zb, lb_b)
    return _heads(jax.nn.silu(q)), _heads(v), gf, kf, gb, kb, gate


def _hgrn_readout(o, gate, norm_g, w_out):
    o = _rmsnorm(o, norm_g)
    o = o.reshape(o.shape[0], o.shape[1], E_WIDTH).astype(gate.dtype)
    return (o * jax.nn.silu(gate)) @ w_out


def _hgrn2_mixer(h, hc, w_in, lb_f, lb_b, norm_g, w_out, need_ctx):
    q, v, gf, kf, gb, kb, gate = _hgrn_project(h, w_in, lb_f, lb_b)
    qc, vc, gfc, kfc, gbc, kbc, gatec = _hgrn_project(hc, w_in, lb_f, lb_b)
    s0 = jnp.zeros((hc.shape[0], N_HEADS, HEAD_K, HEAD_V), jnp.float32)
    oc_f, sc_f = _chunk_gla(qc, kfc, vc, gfc, s0)
    oc_b, sc_b = _chunk_gla_rev(qc, kbc, vc, gbc, s0)
    o_f, _ = _chunk_gla(q, kf, v, gf, sc_f)
    o_b, _ = _chunk_gla_rev(q, kb, v, gb, sc_b)
    y = _hgrn_readout(o_f + o_b, gate, norm_g, w_out)
    yc = _hgrn_readout(oc_f + oc_b, gatec, norm_g, w_out) if need_ctx else None
    return y, yc


def _dwconv(u, w, b):
    pad = CONV_W // 2
    out = lax.conv_general_dilated(u, w[:, None, :], window_strides=(1,), padding=[(pad, pad)],
                                   dimension_numbers=('NWC', 'WIO', 'NWC'),
                                   feature_group_count=u.shape[-1])
    return out + b


def _latent_dwconv(u, w, b, vertical):
    b_, length, ch = u.shape
    rows = length // GRID_W
    grid = u.reshape(b_, rows, GRID_W, ch)
    if vertical:
        grid = grid.transpose(0, 2, 1, 3)
    n1, n2 = grid.shape[1], grid.shape[2]
    out = _dwconv(grid.reshape(b_ * n1, n2, ch), w, b).reshape(b_, n1, n2, ch)
    if vertical:
        out = out.transpose(0, 2, 1, 3)
    return out.reshape(b_, length, ch)


def _conv_mixer(h, w_in, b_in, dw, dw_b, ln_g, ln_b, w_out, b_out, conv_fn):
    p = h @ w_in + b_in
    a, gl, gate = jnp.split(p, 3, axis=-1)
    u = a * jax.nn.sigmoid(gl)
    u = conv_fn(u, dw, dw_b)
    u = jax.nn.silu(_layernorm(u, ln_g, ln_b))
    return (u * jax.nn.silu(gate)) @ w_out + b_out


def setup_inputs(seed: int = 0) -> dict:
    key = jax.random.key(seed)
    ks = jax.random.split(key, 20)
    n_a = (DEPTH + 1) // 2
    n_b = DEPTH // 2
    f32 = jnp.float32

    def w(k, shape, fan_in):
        return jax.random.normal(k, shape, f32) * fan_in ** -0.5

    def small(k, shape, s=0.02):
        return jax.random.normal(k, shape, f32) * s

    return {
        "x": jax.random.normal(ks[0], (BATCH, SEQ, D_MODEL), f32),
        "c": jax.random.normal(ks[1], (BATCH, D_MODEL), f32),
        "ctx": jax.random.normal(ks[2], (BATCH, CTX_LEN, D_MODEL), f32),
        "c_ctx": jax.random.normal(ks[3], (D_MODEL,), f32),
        "norm_g": 1.0 + small(ks[4], (DEPTH, D_MODEL)),
        "ada_w": w(ks[5], (DEPTH, D_MODEL, 3 * D_MODEL), D_MODEL),
        "ada_b": small(ks[6], (DEPTH, 3 * D_MODEL)),
        "hgrn_w_in": w(ks[7], (n_a, D_MODEL, 5 * E_WIDTH), D_MODEL),
        "hgrn_lb_logits": small(ks[8], (2, n_a, E_WIDTH), 0.1),
        "hgrn_norm_g": 1.0 + small(ks[9], (n_a, HEAD_V)),
        "hgrn_w_out": w(ks[10], (n_a, E_WIDTH, D_MODEL), E_WIDTH),
        "conv_w_in": w(ks[11], (n_b, D_MODEL, 3 * E_WIDTH), D_MODEL),
        "conv_b_in": small(ks[12], (n_b, 3 * E_WIDTH)),
        "conv_dw": w(ks[13], (n_b, CONV_W, E_WIDTH), CONV_W),
        "conv_dw_b": small(ks[14], (n_b, E_WIDTH)),
        "conv_ln_g": 1.0 + small(ks[15], (n_b, E_WIDTH)),
        "conv_ln_b": small(ks[16], (n_b, E_WIDTH)),
        "conv_w_out": w(ks[17], (n_b, E_WIDTH, D_MODEL), E_WIDTH),
        "conv_b_out": small(ks[18], (n_b, D_MODEL)),
        "final_norm_g": 1.0 + small(ks[19], (D_MODEL,)),
    }


def reference(x, c, ctx, c_ctx, norm_g, ada_w, ada_b, hgrn_w_in, hgrn_lb_logits, hgrn_norm_g, hgrn_w_out,
              conv_w_in, conv_b_in, conv_dw, conv_dw_b, conv_ln_g, conv_ln_b, conv_w_out, conv_b_out,
              final_norm_g):
    lb = _hgrn_lower_bounds(hgrn_lb_logits)
    for i in range(DEPTH):
        mixer = i % N_MIXERS
        j = i // N_MIXERS
        need_ctx = any(m % N_MIXERS == 0 for m in range(i + 1, DEPTH))
        shift, scale, gate = _ada(c, ada_w[i], ada_b[i])
        h = _modulate(x, norm_g[i], shift[:, None], scale[:, None])
        if mixer == 0 or need_ctx:
            shift_c, scale_c, gate_c = _ada(c_ctx, ada_w[i], ada_b[i])
            hc = _modulate(ctx, norm_g[i], shift_c, scale_c)
        if mixer == 0:
            y, yc = _hgrn2_mixer(h, hc, hgrn_w_in[j], lb[0, j], lb[1, j], hgrn_norm_g[j], hgrn_w_out[j], need_ctx)
        else:
            vertical = (j % 2 == 1)
            lat_conv = lambda u, dw, dwb: _latent_dwconv(u, dw, dwb, vertical)
            y = _conv_mixer(h, conv_w_in[j], conv_b_in[j], conv_dw[j], conv_dw_b[j], conv_ln_g[j], conv_ln_b[j],
                            conv_w_out[j], conv_b_out[j], lat_conv)
            yc = (_conv_mixer(hc, conv_w_in[j], conv_b_in[j], conv_dw[j], conv_dw_b[j], conv_ln_g[j], conv_ln_b[j],
                              conv_w_out[j], conv_b_out[j], _dwconv) if need_ctx else None)
        x = x + gate[:, None] * y
        if need_ctx:
            ctx = ctx + gate_c * yc
    return _rmsnorm(x, final_norm_g)
```

```python
import functools

import jax
import jax.numpy as jnp
from jax import lax
from jax.experimental import pallas as pl
from jax.experimental.pallas import tpu as pltpu

F32 = jnp.float32
BF16 = jnp.bfloat16

EPS = 1e-6
GRID_W = 64
HEAD = 128
CONV_W = 31
CONV_PAD = CONV_W // 2
ROWS = 256
GLA_CHUNK = 64
GLA_SUB = 16
CONV_LANES = 256
VCOLS = 16
VMEM_LIMIT_BYTES = 56 * 1024 * 1024


def _cparams(*sem):
    return pltpu.CompilerParams(dimension_semantics=sem, vmem_limit_bytes=VMEM_LIMIT_BYTES)


def _sigmoid(x):
    return 1.0 / (1.0 + jnp.exp(-x))


def _silu(x):
    return x * _sigmoid(x)


def _ada_kernel(c_ref, w_ref, b_ref, o_ref):
    a = _silu(c_ref[...]).astype(BF16)
    o_ref[...] = jnp.dot(a, w_ref[...].astype(BF16), preferred_element_type=F32) + b_ref[...]


def _ada_call(cs, ada_w, ada_b):
    depth, d, n = ada_w.shape
    bp = cs.shape[0]
    tn = min(n, 1024)
    assert n % tn == 0
    return pl.pallas_call(
        _ada_kernel,
        grid=(depth, n // tn),
        in_specs=[
            pl.BlockSpec((bp, d), lambda l, j: (0, 0)),
            pl.BlockSpec((None, d, tn), lambda l, j: (l, 0, j)),
            pl.BlockSpec((None, 1, tn), lambda l, j: (l, 0, j)),
        ],
        out_specs=pl.BlockSpec((None, bp, tn), lambda l, j: (l, 0, j)),
        out_shape=jax.ShapeDtypeStruct((depth, bp, n), F32),
        compiler_params=_cparams("arbitrary", "arbitrary"),
        name="ada",
    )(cs, ada_w, ada_b.reshape(depth, 1, n))


def _lb_kernel(logit_ref, o_ref):
    n_a = logit_ref.shape[1]
    z = [logit_ref[:, i, :] for i in range(n_a)]
    m = functools.reduce(jnp.maximum, z)
    e = [jnp.exp(v - m) for v in z]
    tot = functools.reduce(lambda a, b: a + b, e)
    acc = jnp.zeros_like(tot)
    for i in range(n_a):
        if i > 0:
            acc = acc + e[i] / tot
        o_ref[:, i, :] = acc


def _lb_call(logits):
    return pl.pallas_call(
        _lb_kernel,
        out_shape=jax.ShapeDtypeStruct(logits.shape, F32),
        name="hgrn_lower_bounds",
    )(logits)


def _proj_kernel(*refs, n_w, n_aux, n_ctx_chunks, epilogue):
    x_ref, modc_ref, modl_ref, ng_ref = refs[:4]
    w_refs = refs[4:4 + n_w]
    aux_refs = refs[4 + n_w:4 + n_w + n_aux]
    out_refs = refs[4 + n_w + n_aux:-2]
    h_scr, wb_scr = refs[-2:]
    n_chunks = x_ref.shape[0] // ROWS

    @pl.when(pl.program_id(1) == 0)
    def _():
        def norm_rows(mod_ref):
            def body(r, carry):
                rows = pl.ds(pl.multiple_of(r * ROWS, ROWS), ROWS)
                xv = x_ref[rows, :]
                y = xv * lax.rsqrt(jnp.mean(xv * xv, axis=-1, keepdims=True) + EPS) * ng_ref[...]
                h_scr[rows, :] = (y * (1.0 + mod_ref[1:2, :]) + mod_ref[0:1, :]).astype(BF16)
                return carry
            return body
        lax.fori_loop(0, n_ctx_chunks, norm_rows(modc_ref), 0)
        lax.fori_loop(n_ctx_chunks, n_chunks, norm_rows(modl_ref), 0)

    for i, w_ref in enumerate(w_refs):
        wb_scr[i] = w_ref[...].astype(BF16)
    aux = [a[...] for a in aux_refs]

    def body(r, carry):
        rows = pl.ds(pl.multiple_of(r * ROWS, ROWS), ROWS)
        h = h_scr[rows, :]
        accs = [jnp.dot(h, wb_scr[i], preferred_element_type=F32) for i in range(n_w)]
        for o_ref, o in zip(out_refs, epilogue(accs, aux)):
            o_ref[rows, :] = o.astype(o_ref.dtype)
        return carry
    lax.fori_loop(0, n_chunks, body, 0)


def _proj_call(name, xa, modc, modl, norm_g, weights, auxs, epilogue, out_dtypes, n_ctx, tn):
    b, t, d = xa.shape
    n_out = auxs[0].shape[1] if auxs else None
    assert t % ROWS == 0 and n_ctx % ROWS == 0 and n_out % tn == 0
    in_specs = [
        pl.BlockSpec((None, t, d), lambda bi, j: (bi, 0, 0)),
        pl.BlockSpec((None, 3, d), lambda bi, j: (0, 0, 0)),
        pl.BlockSpec((None, 3, d), lambda bi, j: (bi, 0, 0)),
        pl.BlockSpec((1, d), lambda bi, j: (0, 0)),
    ]
    args = [xa, modc, modl, norm_g.reshape(1, d)]
    for w, layer, col0 in weights:
        assert col0 % tn == 0
        in_specs.append(pl.BlockSpec((None, d, tn), functools.partial(
            lambda bi, j, layer, blk0: (layer, 0, blk0 + j), layer=layer, blk0=col0 // tn)))
        args.append(w)
    for a in auxs:
        in_specs.append(pl.BlockSpec((1, tn), lambda bi, j: (0, j)))
        args.append(a)
    kern = functools.partial(_proj_kernel, n_w=len(weights), n_aux=len(auxs),
                             n_ctx_chunks=n_ctx // ROWS, epilogue=epilogue)
    return pl.pallas_call(
        kern,
        grid=(b, n_out // tn),
        in_specs=in_specs,
        out_specs=[pl.BlockSpec((None, t, tn), lambda bi, j: (bi, 0, j)) for _ in out_dtypes],
        out_shape=[jax.ShapeDtypeStruct((b, t, n_out), dt) for dt in out_dtypes],
        scratch_shapes=[pltpu.VMEM((t, d), BF16), pltpu.VMEM((len(weights), d, tn), BF16)],
        compiler_params=_cparams("arbitrary", "arbitrary"),
        name=name,
    )(*args)


def _epi_qvgate(accs, aux):
    q, v, gate = accs
    return _silu(q), v, _silu(gate)


def _epi_forget(accs, aux):
    outs = []
    for z, lb in zip(accs, aux):
        f = lb + (1.0 - lb) * _sigmoid(z)
        outs += [jnp.log(f), 1.0 - f]
    return outs


def _epi_conv(accs, aux):
    a, gl, gate = accs
    ba, bg, bgate = aux
    return (a + ba) * _sigmoid(gl + bg), _silu(gate + bgate)


def _cumsum_rows(g, reverse):
    n = g.shape[0]
    row = lax.broadcasted_iota(jnp.int32, g.shape, 0)
    x = g
    sh = 1
    while sh < n:
        if reverse:
            x = x + jnp.where(row < n - sh, pltpu.roll(x, n - sh, 0), 0.0)
        else:
            x = x + jnp.where(row >= sh, pltpu.roll(x, sh, 0), 0.0)
        sh *= 2
    return x


def _gla_chunk(q, k, v, g, st, reverse):
    c = q.shape[0]
    nt = (((1,), (1,)), ((), ()))
    b = _cumsum_rows(g, reverse)
    b_end = b[0:1, :] if reverse else b[c - 1:c, :]
    qi = (q * jnp.exp(b)).astype(BF16)
    inter = lax.dot_general(qi, st.astype(BF16), nt, preferred_element_type=F32)

    t_idx = lax.broadcasted_iota(jnp.int32, (c, c), 0)
    s_idx = lax.broadcasted_iota(jnp.int32, (c, c), 1)

    def scores(ref_q, ref_k):
        qs = (q * jnp.exp(b - ref_q)).astype(BF16)
        ks = (k * jnp.exp(ref_k - b)).astype(BF16)
        return lax.dot_general(qs, ks, nt, preferred_element_type=F32)

    def per_block(m, fn):
        return jnp.concatenate([fn(i) for i in range(c // m)], axis=0)

    def rows(lo, m):
        return b[lo:lo + m, :]

    def bcast(row, m):
        return jnp.broadcast_to(b[row:row + 1, :], (m, b.shape[1]))

    m = GLA_SUB
    ref_d = per_block(m, lambda i: bcast(i * m + m // 2, m))
    shift = m.bit_length() - 1
    same = (t_idx >> shift) == (s_idx >> shift)
    keep = jnp.logical_and(same, (s_idx >= t_idx) if reverse else (s_idx <= t_idx))
    p = jnp.where(keep, scores(ref_d, ref_d), 0.0)
    while m < c:
        q_par = 0 if reverse else 1
        q_bound = (lambda i: (i + 1) * m) if reverse else (lambda i: i * m - 1)
        k_bound = (lambda i: i * m) if reverse else (lambda i: (i + 1) * m - 1)
        ref_q = per_block(m, lambda i: bcast(q_bound(i), m) if i % 2 == q_par else rows(i * m, m))
        ref_k = per_block(m, lambda i: bcast(k_bound(i), m) if i % 2 != q_par else rows(i * m, m))
        shift = m.bit_length() - 1
        tb, sb = t_idx >> shift, s_idx >> shift
        keep = jnp.logical_and((tb & 1) == q_par, sb == (tb + 1 if reverse else tb - 1))
        p = jnp.where(keep, scores(ref_q, ref_k), p)
        m *= 2
    intra = jnp.dot(p.astype(BF16), v, preferred_element_type=F32)
    ke = (k * jnp.exp(b_end - b)).astype(BF16)
    v_t = v.astype(F32).T.astype(BF16)
    st_new = st * jnp.exp(b_end) + jnp.dot(v_t, ke, preferred_element_type=F32)
    return inter + intra, st_new


def _gla_kernel(q_ref, v_ref, gate_ref, gf_ref, kf_ref, gb_ref, kb_ref, ng_ref, o_ref, of_scr, *, n_ctx):
    t = q_ref.shape[0]
    n_chunks = t // GLA_CHUNK
    n_ctx_chunks = n_ctx // GLA_CHUNK

    def chunk_rows(n):
        return pl.ds(pl.multiple_of(n * GLA_CHUNK, GLA_CHUNK), GLA_CHUNK)

    def fwd(n, st):
        rows = chunk_rows(n)
        o, st = _gla_chunk(q_ref[rows, :].astype(F32), kf_ref[rows, :].astype(F32), v_ref[rows, :],
                           gf_ref[rows, :], st, reverse=False)
        of_scr[rows, :] = o
        return st

    def bwd(first_chunk, count):
        def body(i, st):
            rows = chunk_rows(first_chunk - i)
            o, st = _gla_chunk(q_ref[rows, :].astype(F32), kb_ref[rows, :].astype(F32), v_ref[rows, :],
                               gb_ref[rows, :], st, reverse=True)
            o = o + of_scr[rows, :]
            o = o * lax.rsqrt(jnp.mean(o * o, axis=-1, keepdims=True) + EPS) * ng_ref[...]
            o_ref[rows, :] = (o * gate_ref[rows, :].astype(F32)).astype(o_ref.dtype)
            return st
        return lambda st: lax.fori_loop(0, count, body, st)

    st0 = jnp.zeros((HEAD, HEAD), F32)
    lax.fori_loop(0, n_chunks, fwd, st0)
    st_b = bwd(n_ctx_chunks - 1, n_ctx_chunks)(st0)
    bwd(n_chunks - 1, n_chunks - n_ctx_chunks)(st_b)


def _gla_call(q, v, gate, gf, kf, gb, kb, norm_g, n_ctx):
    b, t, e = q.shape
    assert t % GLA_CHUNK == 0 and n_ctx % GLA_CHUNK == 0
    spec = pl.BlockSpec((None, t, HEAD), lambda bi, h: (bi, 0, h))
    return pl.pallas_call(
        functools.partial(_gla_kernel, n_ctx=n_ctx),
        grid=(b, e // HEAD),
        in_specs=[spec] * 7 + [pl.BlockSpec((1, HEAD), lambda bi, h: (0, 0))],
        out_specs=spec,
        out_shape=jax.ShapeDtypeStruct((b, t, e), BF16),
        scratch_shapes=[pltpu.VMEM((t, HEAD), F32)],
        compiler_params=_cparams("arbitrary", "arbitrary"),
        name="hgrn_recurrence",
    )(q, v, gate, gf, kf, gb, kb, norm_g.reshape(1, HEAD))


def _first_step():
    return jnp.logical_and(pl.program_id(0) == 0, pl.program_id(1) == 0)


def _residual(x, y, mod_ref, fin_ref, final_norm):
    out = x + mod_ref[2:3, :] * y
    if final_norm:
        out = out * lax.rsqrt(jnp.mean(out * out, axis=-1, keepdims=True) + EPS) * fin_ref[...]
    return out


def _out_kernel(og_ref, x_ref, modc_ref, modl_ref, w_ref, b_ref, o_ref, wb_scr, *, n_ctx_tiles):
    @pl.when(_first_step())
    def _():
        wb_scr[...] = w_ref[...].astype(BF16)

    y = jnp.dot(og_ref[...], wb_scr[...], preferred_element_type=F32) + b_ref[...]
    is_ctx = pl.program_id(1) < n_ctx_tiles

    @pl.when(is_ctx)
    def _():
        o_ref[...] = _residual(x_ref[...], y, modc_ref, None, False)

    @pl.when(jnp.logical_not(is_ctx))
    def _():
        o_ref[...] = _residual(x_ref[...], y, modl_ref, None, False)


def _out_call(og, xa, modc, modl, w, layer, bias, n_ctx, skip_ctx):
    b, t, e = og.shape
    d = xa.shape[2]
    n_ctx_tiles = n_ctx // ROWS
    off = n_ctx_tiles if skip_ctx else 0
    n_tiles = t // ROWS - off
    return pl.pallas_call(
        functools.partial(_out_kernel, n_ctx_tiles=0 if skip_ctx else n_ctx_tiles),
        grid=(b, n_tiles),
        in_specs=[
            pl.BlockSpec((None, ROWS, e), lambda bi, i: (bi, i + off, 0)),
            pl.BlockSpec((None, ROWS, d), lambda bi, i: (bi, i + off, 0)),
            pl.BlockSpec((None, 3, d), lambda bi, i: (0, 0, 0)),
            pl.BlockSpec((None, 3, d), lambda bi, i: (bi, 0, 0)),
            pl.BlockSpec((None, e, d), lambda bi, i: (layer, 0, 0)),
            pl.BlockSpec((1, d), lambda bi, i: (0, 0)),
        ],
        out_specs=pl.BlockSpec((None, ROWS, d), lambda bi, i: (bi, i, 0)),
        out_shape=jax.ShapeDtypeStruct((b, n_tiles * ROWS, d), F32),
        scratch_shapes=[pltpu.VMEM((e, d), BF16)],
        compiler_params=_cparams("arbitrary", "arbitrary"),
        name="hgrn_out_proj",
    )(og, xa, modc, modl, w, bias)


def _ln_gate_project(c_scr, gate_slab, lng_ref, lnb_ref, wb_scr, n_slabs, rows):
    e = n_slabs * CONV_LANES
    s1 = jnp.zeros((rows, 1), F32)
    for cc in range(n_slabs):
        s1 = s1 + jnp.sum(c_scr[cc], axis=-1, keepdims=True)
    mean = s1 / e
    s2 = jnp.zeros((rows, 1), F32)
    for cc in range(n_slabs):
        dv = c_scr[cc] - mean
        s2 = s2 + jnp.sum(dv * dv, axis=-1, keepdims=True)
    rstd = lax.rsqrt(s2 / e + EPS)
    y = None
    for cc in range(n_slabs):
        lanes = slice(cc * CONV_LANES, (cc + 1) * CONV_LANES)
        u = (c_scr[cc] - mean) * rstd * lng_ref[:, lanes] + lnb_ref[:, lanes]
        u = (_silu(u) * gate_slab(lanes)).astype(BF16)
        part = jnp.dot(u, wb_scr[lanes, :], preferred_element_type=F32)
        y = part if y is None else y + part
    return y


def _convh_kernel(u_ref, gate_ref, x_ref, modc_ref, modl_ref, dw_ref, dwb_ref, lng_ref, lnb_ref,
                  w_ref, b_ref, o_ref, wb_scr, pad_scr, c_scr, *, n_ctx_tiles):
    @pl.when(_first_step())
    def _():
        wb_scr[...] = w_ref[...].astype(BF16)

    n_slabs = pad_scr.shape[0]
    n_rows = ROWS // GRID_W
    halo = 16
    is_ctx = pl.program_id(1) < n_ctx_tiles

    for cc in range(n_slabs):
        lanes = slice(cc * CONV_LANES, (cc + 1) * CONV_LANES)
        for r in range(n_rows):
            pad_scr[cc, r, halo:halo + GRID_W, :] = u_ref[r * GRID_W:(r + 1) * GRID_W, lanes].astype(F32)

    zeros = jnp.zeros((halo, CONV_LANES), F32)

    @pl.when(is_ctx)
    def _():
        for cc in range(n_slabs):
            lanes = slice(cc * CONV_LANES, (cc + 1) * CONV_LANES)
            for r in range(n_rows):
                lo, hi = r * GRID_W, (r + 1) * GRID_W
                pad_scr[cc, r, 0:halo, :] = zeros if r == 0 else u_ref[lo - halo:lo, lanes].astype(F32)
                pad_scr[cc, r, halo + GRID_W:, :] = (
                    zeros if r == n_rows - 1 else u_ref[hi:hi + halo, lanes].astype(F32))

    @pl.when(jnp.logical_not(is_ctx))
    def _():
        for cc in range(n_slabs):
            for r in range(n_rows):
                pad_scr[cc, r, 0:halo, :] = zeros
                pad_scr[cc, r, halo + GRID_W:, :] = zeros

    def conv_slab(cc, carry):
        w = dw_ref[cc]
        for r in range(n_rows):
            acc = jnp.zeros((GRID_W, CONV_LANES), F32)
            for k in range(CONV_W):
                start = halo - CONV_PAD + k
                acc = acc + pad_scr[cc, r, start:start + GRID_W, :] * w[k:k + 1, :]
            c_scr[cc, r * GRID_W:(r + 1) * GRID_W, :] = acc + dwb_ref[cc]
        return carry
    lax.fori_loop(0, n_slabs, conv_slab, 0)

    y = _ln_gate_project(c_scr, lambda lanes: gate_ref[:, lanes].astype(F32), lng_ref, lnb_ref,
                         wb_scr, n_slabs, ROWS) + b_ref[...]

    @pl.when(is_ctx)
    def _():
        o_ref[...] = _residual(x_ref[...], y, modc_ref, None, False)

    @pl.when(jnp.logical_not(is_ctx))
    def _():
        o_ref[...] = _residual(x_ref[...], y, modl_ref, None, False)


def _convv_kernel(u_ref, gate_ref, x_ref, modl_ref, dw_ref, dwb_ref, lng_ref, lnb_ref,
                  w_ref, b_ref, fin_ref, o_ref, wb_scr, us_scr, c_scr):
    @pl.when(_first_step())
    def _():
        wb_scr[...] = w_ref[...].astype(BF16)

    n_slabs = c_scr.shape[0]
    n_grid_rows = u_ref.shape[0]
    tokens = n_grid_rows * VCOLS

    for cc in range(n_slabs):
        us_scr[cc] = u_ref[:, :, cc * CONV_LANES:(cc + 1) * CONV_LANES].astype(F32)

    def conv_slab(cc, carry):
        w = dw_ref[cc]
        for r in range(n_grid_rows):
            acc = jnp.zeros((VCOLS, CONV_LANES), F32)
            for k in range(CONV_W):
                src = r + k - CONV_PAD
                if 0 <= src < n_grid_rows:
                    acc = acc + us_scr[cc, src] * w[k:k + 1, :]
            c_scr[cc, r * VCOLS:(r + 1) * VCOLS, :] = acc + dwb_ref[cc]
        return carry
    lax.fori_loop(0, n_slabs, conv_slab, 0)

    def gate_slab(lanes):
        return gate_ref[:, :, lanes].astype(F32).reshape(tokens, CONV_LANES)

    y = _ln_gate_project(c_scr, gate_slab, lng_ref, lnb_ref, wb_scr, n_slabs, tokens) + b_ref[...]
    d = x_ref.shape[-1]
    out = _residual(x_ref[...].reshape(tokens, d), y, modl_ref, fin_ref, True)
    o_ref[...] = out.reshape(n_grid_rows, VCOLS, d)


def _slab_params(dw, dw_b):
    e = dw.shape[1]
    n_slabs = e // CONV_LANES
    dw_s = dw.reshape(CONV_W, n_slabs, CONV_LANES).transpose(1, 0, 2)
    return dw_s, dw_b.reshape(n_slabs, 1, CONV_LANES), n_slabs


def _convh_call(u, gate, xa, modc, modl, dw, dw_b, ln_g, ln_b, w, layer, bias, n_ctx):
    b, t, e = u.shape
    d = xa.shape[2]
    assert n_ctx in (0, ROWS) and ROWS % GRID_W == 0
    dw_s, dwb_s, n_slabs = _slab_params(dw, dw_b)
    tile = lambda width: pl.BlockSpec((None, ROWS, width), lambda bi, i: (bi, i, 0))
    full2 = lambda a: pl.BlockSpec(a.shape, lambda bi, i: (0, 0))
    full3 = lambda a: pl.BlockSpec(a.shape, lambda bi, i: (0, 0, 0))
    ln_g, ln_b, bias = ln_g.reshape(1, e), ln_b.reshape(1, e), bias.reshape(1, d)
    return pl.pallas_call(
        functools.partial(_convh_kernel, n_ctx_tiles=n_ctx // ROWS),
        grid=(b, t // ROWS),
        in_specs=[
            tile(e), tile(e), tile(d),
            pl.BlockSpec((None, 3, d), lambda bi, i: (0, 0, 0)),
            pl.BlockSpec((None, 3, d), lambda bi, i: (bi, 0, 0)),
            full3(dw_s), full3(dwb_s), full2(ln_g), full2(ln_b),
            pl.BlockSpec((None, e, d), lambda bi, i: (layer, 0, 0)),
            full2(bias),
        ],
        out_specs=tile(d),
        out_shape=jax.ShapeDtypeStruct((b, t, d), F32),
        scratch_shapes=[
            pltpu.VMEM((e, d), BF16),
            pltpu.VMEM((n_slabs, ROWS // GRID_W, GRID_W + 32, CONV_LANES), F32),
            pltpu.VMEM((n_slabs, ROWS, CONV_LANES), F32),
        ],
        compiler_params=_cparams("arbitrary", "arbitrary"),
        name="conv_rows_out_proj",
    )(u, gate, xa, modc, modl, dw_s, dwb_s, ln_g, ln_b, w, bias)


def _convv_call(u, gate, x, modl, dw, dw_b, ln_g, ln_b, w, layer, bias, final_g):
    b, l, e = u.shape
    d = x.shape[2]
    n_grid_rows = l // GRID_W
    dw_s, dwb_s, n_slabs = _slab_params(dw, dw_b)
    grid4 = lambda a: a.reshape(b, n_grid_rows, GRID_W, a.shape[-1])
    tile = lambda width: pl.BlockSpec((None, n_grid_rows, VCOLS, width), lambda bi, i: (bi, 0, i, 0))
    full2 = lambda a: pl.BlockSpec(a.shape, lambda bi, i: (0, 0))
    full3 = lambda a: pl.BlockSpec(a.shape, lambda bi, i: (0, 0, 0))
    ln_g, ln_b, bias, final_g = ln_g.reshape(1, e), ln_b.reshape(1, e), bias.reshape(1, d), final_g.reshape(1, d)
    out = pl.pallas_call(
        _convv_kernel,
        grid=(b, GRID_W // VCOLS),
        in_specs=[
            tile(e), tile(e), tile(d),
            pl.BlockSpec((None, 3, d), lambda bi, i: (bi, 0, 0)),
            full3(dw_s), full3(dwb_s), full2(ln_g), full2(ln_b),
            pl.BlockSpec((None, e, d), lambda bi, i: (layer, 0, 0)),
            full2(bias), full2(final_g),
        ],
        out_specs=tile(d),
        out_shape=jax.ShapeDtypeStruct((b, n_grid_rows, GRID_W, d), F32),
        scratch_shapes=[
            pltpu.VMEM((e, d), BF16),
            pltpu.VMEM((n_slabs, n_grid_rows, VCOLS, CONV_LANES), F32),
            pltpu.VMEM((n_slabs, n_grid_rows * VCOLS, CONV_LANES), F32),
        ],
        compiler_params=_cparams("arbitrary", "arbitrary"),
        name="conv_cols_out_proj_final_norm",
    )(grid4(u), grid4(gate), grid4(x), modl, dw_s, dwb_s, ln_g, ln_b, w, bias, final_g)
    return out.reshape(b, l, d)


def _hgrn_layer(xa, modc, modl, norm_g, w_in, lb, hnorm_g, w_out, j, n_ctx, last_recurrent):
    e = w_out.shape[1]
    d = xa.shape[2]
    none = jnp.zeros((1, e), F32)
    q, v, gate = _proj_call("hgrn_proj_qvg", xa, modc, modl, norm_g,
                            [(w_in, j, 0), (w_in, j, e), (w_in, j, 4 * e)], [none],
                            _epi_qvgate, [BF16] * 3, n_ctx, tn=256)
    gf, kf, gb, kb = _proj_call("hgrn_proj_forget", xa, modc, modl, norm_g,
                                [(w_in, j, 2 * e), (w_in, j, 3 * e)],
                                [lb[0, j].reshape(1, e), lb[1, j].reshape(1, e)],
                                _epi_forget, [F32, BF16, F32, BF16], n_ctx, tn=256)
    og = _gla_call(q, v, gate, gf, kf, gb, kb, hnorm_g[j], n_ctx)
    return _out_call(og, xa, modc, modl, w_out, j, jnp.zeros((1, d), F32), n_ctx, skip_ctx=last_recurrent)


def _conv_proj(xa, modc, modl, norm_g, w_in, b_in, j, n_ctx):
    e = w_in.shape[2] // 3
    bias = b_in[j].reshape(1, 3 * e)
    return _proj_call("conv_proj", xa, modc, modl, norm_g,
                      [(w_in, j, 0), (w_in, j, e), (w_in, j, 2 * e)],
                      [bias[:, :e], bias[:, e:2 * e], bias[:, 2 * e:]],
                      _epi_conv, [BF16, BF16], n_ctx, tn=256)


def kernel(x, c, ctx, c_ctx, norm_g, ada_w, ada_b, hgrn_w_in, hgrn_lb_logits, hgrn_norm_g, hgrn_w_out,
           conv_w_in, conv_b_in, conv_dw, conv_dw_b, conv_ln_g, conv_ln_b, conv_w_out, conv_b_out,
           final_norm_g):
    b, l, d = x.shape
    n_ctx = ctx.shape[1]
    depth = norm_g.shape[0]
    assert depth == 4, "layer schedule below is written for the 4-layer hybrid"

    bp = -(-(b + 1) // 8) * 8
    cs = jnp.concatenate([c, c_ctx[None, :], jnp.zeros((bp - b - 1, d), F32)], axis=0)
    mod = _ada_call(cs, ada_w, ada_b).reshape(depth, bp, 3, d)
    lb = _lb_call(hgrn_lb_logits)
    xa = jnp.concatenate([ctx, x], axis=1)

    def mods(i):
        return mod[i, b:b + 1], mod[i, :b]

    modc, modl = mods(0)
    xa = _hgrn_layer(xa, modc, modl, norm_g[0], hgrn_w_in, lb, hgrn_norm_g, hgrn_w_out, 0, n_ctx, False)
    modc, modl = mods(1)
    u, gate = _conv_proj(xa, modc, modl, norm_g[1], conv_w_in, conv_b_in, 0, n_ctx)
    xa = _convh_call(u, gate, xa, modc, modl, conv_dw[0], conv_dw_b[0], conv_ln_g[0], conv_ln_b[0],
                     conv_w_out, 0, conv_b_out[0], n_ctx)
    modc, modl = mods(2)
    xl = _hgrn_layer(xa, modc, modl, norm_g[2], hgrn_w_in, lb, hgrn_norm_g, hgrn_w_out, 1, n_ctx, True)
    modc, modl = mods(3)
    u, gate = _conv_proj(xl, modc, modl, norm_g[3], conv_w_in, conv_b_in, 1, 0)
    return _convv_call(u, gate, xl, modl, conv_dw[1], conv_dw_b[1], conv_ln_g[1], conv_ln_b[1],
                       conv_w_out, 1, conv_b_out[1], final_norm_g)
```

```python
import functools

import jax
import jax.numpy as jnp
from jax import lax
from jax.experimental import pallas as pl
from jax.experimental.pallas import tpu as pltpu

F32 = jnp.float32
BF16 = jnp.bfloat16

EPS = 1e-6
GRID_W = 64
HEAD = 128
CONV_W = 31
CONV_PAD = CONV_W // 2
ROWS = 256
GLA_CHUNK = 64
GLA_SUB = 16
SUBLANES = 8
CONV_LANES = 256
VCOLS = 16
VMEM_LIMIT_BYTES = 56 * 1024 * 1024


def _cparams(*sem):
    return pltpu.CompilerParams(dimension_semantics=sem, vmem_limit_bytes=VMEM_LIMIT_BYTES)


def _sigmoid(x):
    return 1.0 / (1.0 + jnp.exp(-x))


def _silu(x):
    return x * _sigmoid(x)


def _ada_kernel(c_ref, w_ref, b_ref, o_ref):
    a = _silu(c_ref[...]).astype(BF16)
    o_ref[...] = jnp.dot(a, w_ref[...].astype(BF16), preferred_element_type=F32) + b_ref[...]


def _ada_call(cs, ada_w, ada_b):
    depth, d, n = ada_w.shape
    bp = cs.shape[0]
    tn = min(n, 1024)
    assert n % tn == 0
    return pl.pallas_call(
        _ada_kernel,
        grid=(depth, n // tn),
        in_specs=[
            pl.BlockSpec((bp, d), lambda l, j: (0, 0)),
            pl.BlockSpec((None, d, tn), lambda l, j: (l, 0, j)),
            pl.BlockSpec((None, 1, tn), lambda l, j: (l, 0, j)),
        ],
        out_specs=pl.BlockSpec((None, bp, tn), lambda l, j: (l, 0, j)),
        out_shape=jax.ShapeDtypeStruct((depth, bp, n), F32),
        compiler_params=_cparams("arbitrary", "arbitrary"),
        name="ada",
    )(cs, ada_w, ada_b.reshape(depth, 1, n))


def _lb_kernel(logit_ref, o_ref):
    n_a = logit_ref.shape[1]
    z = [logit_ref[:, i, :] for i in range(n_a)]
    m = functools.reduce(jnp.maximum, z)
    e = [jnp.exp(v - m) for v in z]
    tot = functools.reduce(lambda a, b: a + b, e)
    acc = jnp.zeros_like(tot)
    for i in range(n_a):
        if i > 0:
            acc = acc + e[i] / tot
        o_ref[:, i, :] = acc


def _lb_call(logits):
    return pl.pallas_call(
        _lb_kernel,
        out_shape=jax.ShapeDtypeStruct(logits.shape, F32),
        name="hgrn_lower_bounds",
    )(logits)


def _proj_kernel(*refs, n_w, n_aux, n_ctx_chunks, epilogue):
    x_ref, modc_ref, modl_ref, ng_ref = refs[:4]
    w_refs = refs[4:4 + n_w]
    aux_refs = refs[4 + n_w:4 + n_w + n_aux]
    out_refs = refs[4 + n_w + n_aux:-1]
    h_scr = refs[-1]
    n_chunks = x_ref.shape[0] // ROWS

    @pl.when(pl.program_id(1) == 0)
    def _():
        def norm_rows(mod_ref):
            def body(r, carry):
                rows = pl.ds(pl.multiple_of(r * ROWS, ROWS), ROWS)
                xv = x_ref[rows, :]
                y = xv * lax.rsqrt(jnp.mean(xv * xv, axis=-1, keepdims=True) + EPS) * ng_ref[...]
                h_scr[rows, :] = (y * (1.0 + mod_ref[1:2, :]) + mod_ref[0:1, :]).astype(BF16)
                return carry
            return body
        lax.fori_loop(0, n_ctx_chunks, norm_rows(modc_ref), 0)
        lax.fori_loop(n_ctx_chunks, n_chunks, norm_rows(modl_ref), 0)

    aux = [a[...] for a in aux_refs]

    def body(r, carry):
        rows = pl.ds(pl.multiple_of(r * ROWS, ROWS), ROWS)
        h = h_scr[rows, :]
        accs = [jnp.dot(h, w_ref[...], preferred_element_type=F32) for w_ref in w_refs]
        for o_ref, o in zip(out_refs, epilogue(accs, aux)):
            o_ref[rows, :] = o.astype(o_ref.dtype)
        return carry
    lax.fori_loop(0, n_chunks, body, 0)


def _proj_call(name, xa, modc, modl, norm_g, weights, auxs, epilogue, out_dtypes, n_out, n_ctx, tn):
    b, t, d = xa.shape
    assert t % ROWS == 0 and n_ctx % ROWS == 0 and n_out % tn == 0
    in_specs = [
        pl.BlockSpec((None, t, d), lambda bi, j: (bi, 0, 0)),
        pl.BlockSpec((None, 3, d), lambda bi, j: (0, 0, 0)),
        pl.BlockSpec((None, 3, d), lambda bi, j: (bi, 0, 0)),
        pl.BlockSpec((1, d), lambda bi, j: (0, 0)),
    ]
    args = [xa, modc, modl, norm_g.reshape(1, d)]
    for w, layer, col0 in weights:
        assert col0 % tn == 0
        in_specs.append(pl.BlockSpec((None, d, tn), functools.partial(
            lambda bi, j, layer, blk0: (layer, 0, blk0 + j), layer=layer, blk0=col0 // tn)))
        args.append(w)
    for a in auxs:
        in_specs.append(pl.BlockSpec((1, tn), lambda bi, j: (0, j)))
        args.append(a)
    kern = functools.partial(_proj_kernel, n_w=len(weights), n_aux=len(auxs),
                             n_ctx_chunks=n_ctx // ROWS, epilogue=epilogue)
    return pl.pallas_call(
        kern,
        grid=(b, n_out // tn),
        in_specs=in_specs,
        out_specs=[pl.BlockSpec((None, t, tn), lambda bi, j: (bi, 0, j)) for _ in out_dtypes],
        out_shape=[jax.ShapeDtypeStruct((b, t, n_out), dt) for dt in out_dtypes],
        scratch_shapes=[pltpu.VMEM((t, d), BF16)],
        compiler_params=_cparams("arbitrary", "arbitrary"),
        name=name,
    )(*args)


def _epi_qvgate(accs, aux):
    q, v, gate = accs
    return _silu(q), v, _silu(gate)


def _epi_forget(accs, aux):
    outs = []
    for z, lb in zip(accs, aux):
        f = lb + (1.0 - lb) * _sigmoid(z)
        outs += [jnp.log2(f), 1.0 - f]
    return outs


def _epi_conv(accs, aux):
    a, gl, gate = accs
    ba, bg, bgate = aux
    return (a + ba) * _sigmoid(gl + bg), _silu(gate + bgate)


def _cumsum_rows(g, reverse):
    n, width = g.shape
    row = lax.broadcasted_iota(jnp.int32, (SUBLANES, width), 0)
    groups = [g[j:j + SUBLANES, :] for j in range(0, n, SUBLANES)]
    sh = 1
    while sh < SUBLANES:
        if reverse:
            groups = [x + jnp.where(row < SUBLANES - sh, pltpu.roll(x, SUBLANES - sh, 0), 0.0) for x in groups]
        else:
            groups = [x + jnp.where(row >= sh, pltpu.roll(x, sh, 0), 0.0) for x in groups]
        sh *= 2
    out, carry = [], None
    for x in (reversed(groups) if reverse else groups):
        if carry is not None:
            x = x + carry
        carry = x[0:1, :] if reverse else x[SUBLANES - 1:SUBLANES, :]
        out.append(x)
    return jnp.concatenate(out[::-1] if reverse else out, axis=0)


_NT = (((1,), (1,)), ((), ()))


def _gla_chunk(q, k, g2, masks, reverse):
    c, width = q.shape
    b = _cumsum_rows(g2, reverse)
    b_end = b[0:1, :] if reverse else b[c - 1:c, :]
    qi = (q * jnp.exp2(b)).astype(BF16)
    ke = (k * jnp.exp2(b_end - b)).astype(BF16)

    def scaled(x, m, is_query, ref_row):
        out = []
        for i in range(c // m):
            r = ref_row(i)
            if r is None:
                out.append(jnp.zeros((m, width), BF16))
            else:
                blk = slice(i * m, (i + 1) * m)
                e = b[blk, :] - b[r:r + 1, :] if is_query else b[r:r + 1, :] - b[blk, :]
                out.append((x[blk, :] * jnp.exp2(e)).astype(BF16))
        return jnp.concatenate(out, axis=0)

    def scores(m, q_row, k_row):
        return lax.dot_general(scaled(q, m, True, q_row), scaled(k, m, False, k_row), _NT,
                               preferred_element_type=F32)

    m = GLA_SUB
    mid = lambda i: i * m + m // 2
    p = jnp.where(masks[0], scores(m, mid, mid), 0.0)
    level = 1
    while m < c:
        q_par = 0 if reverse else 1
        q_bound = (lambda i: (i + 1) * m) if reverse else (lambda i: i * m - 1)
        k_bound = (lambda i: i * m) if reverse else (lambda i: (i + 1) * m - 1)
        s = scores(m, lambda i: q_bound(i) if i % 2 == q_par else None,
                   lambda i: k_bound(i) if i % 2 != q_par else None)
        p = p + s if 2 * m == c else jnp.where(masks[level], s, p)
        m *= 2
        level += 1
    return p, qi, ke, jnp.exp2(b_end)


def _gla_masks(c, reverse):
    t_idx = lax.broadcasted_iota(jnp.int32, (c, c), 0)
    s_idx = lax.broadcasted_iota(jnp.int32, (c, c), 1)
    m = GLA_SUB
    shift = m.bit_length() - 1
    same = (t_idx >> shift) == (s_idx >> shift)
    masks = [jnp.logical_and(same, (s_idx >= t_idx) if reverse else (s_idx <= t_idx))]
    while 2 * m < c:
        tb, sb = t_idx >> shift, s_idx >> shift
        q_par = 0 if reverse else 1
        masks.append(jnp.logical_and((tb & 1) == q_par, sb == (tb + 1 if reverse else tb - 1)))
        m *= 2
        shift += 1
    return masks


def _gla_kernel(q_ref, v_ref, gate_ref, gf_ref, kf_ref, gb_ref, kb_ref, ng_ref, o_ref,
                oi_scr, qi_scr, u_scr, d_scr, st_scr, *, n_ctx):
    t = q_ref.shape[0]
    n_chunks = t // GLA_CHUNK
    n_ctx_chunks = n_ctx // GLA_CHUNK
    fwd_lanes, bwd_lanes = slice(0, HEAD), slice(HEAD, 2 * HEAD)

    def chunk_rows(n):
        return pl.ds(pl.multiple_of(n * GLA_CHUNK, GLA_CHUNK), GLA_CHUNK)

    masks_f = _gla_masks(GLA_CHUNK, reverse=False)
    masks_b = _gla_masks(GLA_CHUNK, reverse=True)

    def local(n, carry):
        rows = chunk_rows(n)
        q = q_ref[rows, :].astype(F32)
        v = v_ref[rows, :]
        p_f, qi_f, ke_f, d_f = _gla_chunk(q, kf_ref[rows, :].astype(F32), gf_ref[rows, :], masks_f, False)
        p_b, qi_b, ke_b, d_b = _gla_chunk(q, kb_ref[rows, :].astype(F32), gb_ref[rows, :], masks_b, True)
        oi_scr[rows, :] = jnp.dot((p_f + p_b).astype(BF16), v, preferred_element_type=F32)
        qi_scr[rows, :] = jnp.concatenate([qi_f, qi_b], axis=1)
        v_t = v.astype(F32).T.astype(BF16)
        u_scr[n] = jnp.dot(v_t, jnp.concatenate([ke_f, ke_b], axis=1), preferred_element_type=F32)
        d_scr[n] = jnp.concatenate([d_f, d_b], axis=1)
        return carry
    lax.fori_loop(0, n_chunks, local, 0, unroll=4)

    def scan(lanes, first_chunk, step, count, st):
        def body(i, st):
            n = first_chunk + step * i
            st_scr[n, :, lanes] = st.astype(BF16)
            return st * d_scr[n, :, lanes] + u_scr[n, :, lanes]
        return lax.fori_loop(0, count, body, st)

    st0 = jnp.zeros((HEAD, HEAD), F32)
    scan(fwd_lanes, 0, 1, n_chunks, st0)
    st_b = scan(bwd_lanes, n_ctx_chunks - 1, -1, n_ctx_chunks, st0)
    scan(bwd_lanes, n_chunks - 1, -1, n_chunks - n_ctx_chunks, st_b)

    def readout(n, carry):
        rows = chunk_rows(n)
        o = oi_scr[rows, :] + lax.dot_general(qi_scr[rows, :], st_scr[n], _NT, preferred_element_type=F32)
        o = o * lax.rsqrt(jnp.mean(o * o, axis=-1, keepdims=True) + EPS) * ng_ref[...]
        o_ref[rows, :] = (o * gate_ref[rows, :].astype(F32)).astype(o_ref.dtype)
        return carry
    lax.fori_loop(0, n_chunks, readout, 0, unroll=12)


def _gla_call(q, v, gate, gf, kf, gb, kb, norm_g, n_ctx):
    b, t, e = q.shape
    assert t % GLA_CHUNK == 0 and n_ctx % GLA_CHUNK == 0
    n_chunks = t // GLA_CHUNK
    spec =pl.BlockSpec((None, t, HEAD), lambda bi, h: (bi, 0, h))
    return pl.pallas_call(
        functools.partial(_gla_kernel, n_ctx=n_ctx),
        grid=(b, e // HEAD),
        in_specs=[spec] * 7 + [pl.BlockSpec((1, HEAD), lambda bi, h: (0, 0))],
        out_specs=spec,
        out_shape=jax.ShapeDtypeStruct((b, t, e), BF16),
        scratch_shapes=[
            pltpu.VMEM((t, HEAD), F32),
            pltpu.VMEM((t, 2 * HEAD), BF16),
            pltpu.VMEM((n_chunks, HEAD, 2 * HEAD), F32),
            pltpu.VMEM((n_chunks, 1, 2 * HEAD), F32),
            pltpu.VMEM((n_chunks, HEAD, 2 * HEAD), BF16),
        ],
        compiler_params=_cparams("arbitrary", "arbitrary"),
        name="hgrn_recurrence",
    )(q, v, gate, gf, kf, gb, kb, norm_g.reshape(1, HEAD))


def _residual(x, y, mod_ref, fin_ref, final_norm):
    out = x + mod_ref[2:3, :] * y
    if final_norm:
        out = out * lax.rsqrt(jnp.mean(out * out, axis=-1, keepdims=True) + EPS) * fin_ref[...]
    return out


def _out_kernel(og_ref, x_ref, modc_ref, modl_ref, w_ref, b_ref, o_ref, *, n_ctx_tiles):
    y = jnp.dot(og_ref[...], w_ref[...], preferred_element_type=F32) + b_ref[...]
    is_ctx = pl.program_id(1) < n_ctx_tiles

    @pl.when(is_ctx)
    def _():
        o_ref[...] = _residual(x_ref[...], y, modc_ref, None, False)

    @pl.when(jnp.logical_not(is_ctx))
    def _():
        o_ref[...] = _residual(x_ref[...], y, modl_ref, None, False)


def _out_call(og, xa, modc, modl, w, layer, bias, n_ctx, skip_ctx):
    b, t, e = og.shape
    d = xa.shape[2]
    n_ctx_tiles = n_ctx // ROWS
    off = n_ctx_tiles if skip_ctx else 0
    n_tiles = t // ROWS - off
    return pl.pallas_call(
        functools.partial(_out_kernel, n_ctx_tiles=0 if skip_ctx else n_ctx_tiles),
        grid=(b, n_tiles),
        in_specs=[
            pl.BlockSpec((None, ROWS, e), lambda bi, i: (bi, i + off, 0)),
            pl.BlockSpec((None, ROWS, d), lambda bi, i: (bi, i + off, 0)),
            pl.BlockSpec((None, 3, d), lambda bi, i: (0, 0, 0)),
            pl.BlockSpec((None, 3, d), lambda bi, i: (bi, 0, 0)),
            pl.BlockSpec((None, e, d), lambda bi, i: (layer, 0, 0)),
            pl.BlockSpec((1, d), lambda bi, i: (0, 0)),
        ],
        out_specs=pl.BlockSpec((None, ROWS, d), lambda bi, i: (bi, i, 0)),
        out_shape=jax.ShapeDtypeStruct((b, n_tiles * ROWS, d), F32),
        compiler_params=_cparams("arbitrary", "arbitrary"),
        name="hgrn_out_proj",
    )(og, xa, modc, modl, w, bias)


def _ln_gate_project(c_scr, gate_slab, lng_ref, lnb_ref, w_ref, n_slabs, rows):
    e = n_slabs * CONV_LANES
    s1 = jnp.zeros((rows, 1), F32)
    for cc in range(n_slabs):
        s1 = s1 + jnp.sum(c_scr[cc], axis=-1, keepdims=True)
    mean = s1 / e
    s2 = jnp.zeros((rows, 1), F32)
    for cc in range(n_slabs):
        dv = c_scr[cc] - mean
        s2 = s2 + jnp.sum(dv * dv, axis=-1, keepdims=True)
    rstd = lax.rsqrt(s2 / e + EPS)
    y = None
    for cc in range(n_slabs):
        lanes = slice(cc * CONV_LANES, (cc + 1) * CONV_LANES)
        u = (c_scr[cc] - mean) * rstd * lng_ref[:, lanes] + lnb_ref[:, lanes]
        u = (_silu(u) * gate_slab(lanes)).astype(BF16)
        part = jnp.dot(u, w_ref[lanes, :], preferred_element_type=F32)
        y = part if y is None else y + part
    return y


def _convh_kernel(u_ref, gate_ref, x_ref, modc_ref, modl_ref, dw_ref, dwb_ref, lng_ref, lnb_ref,
                  w_ref, b_ref, o_ref, pad_scr, shift_scr, c_scr, *, n_ctx_tiles):
    n_slabs = pad_scr.shape[0]
    n_rows = ROWS // GRID_W
    halo = 16
    is_ctx = pl.program_id(1) < n_ctx_tiles

    for cc in range(n_slabs):
        lanes = slice(cc * CONV_LANES, (cc + 1) * CONV_LANES)
        for r in range(n_rows):
            pad_scr[cc, r, halo:halo + GRID_W, :] = u_ref[r * GRID_W:(r + 1) * GRID_W, lanes].astype(F32)

    zeros = jnp.zeros((halo, CONV_LANES), F32)

    @pl.when(is_ctx)
    def _():
        for cc in range(n_slabs):
            lanes = slice(cc * CONV_LANES, (cc + 1) * CONV_LANES)
            for r in range(n_rows):
                lo, hi = r * GRID_W, (r + 1) * GRID_W
                pad_scr[cc, r, 0:halo, :] = zeros if r == 0 else u_ref[lo - halo:lo, lanes].astype(F32)
                pad_scr[cc, r, halo + GRID_W:, :] = (
                    zeros if r == n_rows - 1 else u_ref[hi:hi + halo, lanes].astype(F32))

    @pl.when(jnp.logical_not(is_ctx))
    def _():
        for cc in range(n_slabs):
            for r in range(n_rows):
                pad_scr[cc, r, 0:halo, :] = zeros
                pad_scr[cc, r, halo + GRID_W:, :] = zeros

    def conv_slab(cc, carry):
        w = dw_ref[cc]
        for r in range(n_rows):
            acc = jnp.zeros((GRID_W, CONV_LANES), F32)
            span = GRID_W + SUBLANES * ((CONV_W + halo - CONV_PAD) // SUBLANES - 1)
            for res in range(SUBLANES):
                shift_scr[res] = pad_scr[cc, r, res:res + span, :]
            for res in range(SUBLANES):
                for off in range(0, span - GRID_W + 1, SUBLANES):
                    k = off + res - (halo - CONV_PAD)
                    if 0 <= k < CONV_W:
                        acc = acc + shift_scr[res, off:off + GRID_W, :] * w[k:k + 1, :]
            c_scr[cc, r * GRID_W:(r + 1) * GRID_W, :] = acc + dwb_ref[cc]
        return carry
    lax.fori_loop(0, n_slabs, conv_slab, 0)

    y = _ln_gate_project(c_scr, lambda lanes: gate_ref[:, lanes].astype(F32), lng_ref, lnb_ref,
                         w_ref, n_slabs, ROWS) + b_ref[...]

    @pl.when(is_ctx)
    def _():
        o_ref[...] = _residual(x_ref[...], y, modc_ref, None, False)

    @pl.when(jnp.logical_not(is_ctx))
    def _():
        o_ref[...] = _residual(x_ref[...], y, modl_ref, None, False)


def _convv_kernel(u_ref, gate_ref, x_ref, modl_ref, dw_ref, dwb_ref, lng_ref, lnb_ref,
                  w_ref, b_ref, fin_ref, o_ref, us_scr, c_scr):
    n_slabs = c_scr.shape[0]
    n_grid_rows = u_ref.shape[0]
    tokens = n_grid_rows * VCOLS

    for cc in range(n_slabs):
        us_scr[cc] = u_ref[:, :, cc * CONV_LANES:(cc + 1) * CONV_LANES].astype(F32)

    def conv_slab(cc, carry):
        w = dw_ref[cc]
        for r in range(n_grid_rows):
            acc = jnp.zeros((VCOLS, CONV_LANES), F32)
            for k in range(CONV_W):
                src = r + k - CONV_PAD
                if 0 <= src < n_grid_rows:
                    acc = acc + us_scr[cc, src] * w[k:k + 1, :]
            c_scr[cc, r * VCOLS:(r + 1) * VCOLS, :] = acc + dwb_ref[cc]
        return carry
    lax.fori_loop(0, n_slabs, conv_slab, 0)

    def gate_slab(lanes):
        return gate_ref[:, :, lanes].astype(F32).reshape(tokens, CONV_LANES)

    y = _ln_gate_project(c_scr, gate_slab, lng_ref, lnb_ref, w_ref, n_slabs, tokens) + b_ref[...]
    d = x_ref.shape[-1]
    out = _residual(x_ref[...].reshape(tokens, d), y, modl_ref, fin_ref, True)
    o_ref[...] = out.reshape(n_grid_rows, VCOLS, d)


def _slab_params(dw, dw_b):
    e = dw.shape[1]
    n_slabs = e // CONV_LANES
    dw_s = dw.reshape(CONV_W, n_slabs, CONV_LANES).transpose(1, 0, 2)
    return dw_s, dw_b.reshape(n_slabs, 1, CONV_LANES), n_slabs


def _convh_call(u, gate, xa, modc, modl, dw, dw_b, ln_g, ln_b, w, layer, bias, n_ctx):
    b, t, e = u.shape
    d = xa.shape[2]
    assert n_ctx in (0, ROWS) and ROWS % GRID_W == 0
    dw_s, dwb_s, n_slabs = _slab_params(dw, dw_b)
    tile = lambda width: pl.BlockSpec((None, ROWS, width), lambda bi, i: (bi, i, 0))
    full2 = lambda a: pl.BlockSpec(a.shape, lambda bi, i: (0, 0))
    full3 = lambda a: pl.BlockSpec(a.shape, lambda bi, i: (0, 0, 0))
    ln_g, ln_b, bias = ln_g.reshape(1, e), ln_b.reshape(1, e), bias.reshape(1, d)
    return pl.pallas_call(
        functools.partial(_convh_kernel, n_ctx_tiles=n_ctx // ROWS),
        grid=(b, t // ROWS),
        in_specs=[
            tile(e), tile(e), tile(d),
            pl.BlockSpec((None, 3, d), lambda bi, i: (0, 0, 0)),
            pl.BlockSpec((None, 3, d), lambda bi, i: (bi, 0, 0)),
            full3(dw_s), full3(dwb_s), full2(ln_g), full2(ln_b),
            pl.BlockSpec((None, e, d), lambda bi, i: (layer, 0, 0)),
            full2(bias),
        ],
        out_specs=tile(d),
        out_shape=jax.ShapeDtypeStruct((b, t, d), F32),
        scratch_shapes=[
            pltpu.VMEM((n_slabs, ROWS // GRID_W, GRID_W + 32, CONV_LANES), F32),
            pltpu.VMEM((SUBLANES, GRID_W + 24, CONV_LANES), F32),
            pltpu.VMEM((n_slabs, ROWS, CONV_LANES), F32),
        ],
        compiler_params=_cparams("arbitrary", "arbitrary"),
        name="conv_rows_out_proj",
    )(u, gate, xa, modc, modl, dw_s, dwb_s, ln_g, ln_b, w, bias)


def _convv_call(u, gate, x, modl, dw, dw_b, ln_g, ln_b, w, layer, bias, final_g):
    b, l, e = u.shape
    d = x.shape[2]
    n_grid_rows = l // GRID_W
    dw_s, dwb_s, n_slabs = _slab_params(dw, dw_b)
    grid4 = lambda a: a.reshape(b, n_grid_rows, GRID_W, a.shape[-1])
    tile = lambda width: pl.BlockSpec((None, n_grid_rows, VCOLS, width), lambda bi, i: (bi, 0, i, 0))
    full2 = lambda a: pl.BlockSpec(a.shape, lambda bi, i: (0, 0))
    full3 = lambda a: pl.BlockSpec(a.shape, lambda bi, i: (0, 0, 0))
    ln_g, ln_b, bias, final_g = ln_g.reshape(1, e), ln_b.reshape(1, e), bias.reshape(1, d), final_g.reshape(1, d)
    out = pl.pallas_call(
        _convv_kernel,
        grid=(b, GRID_W // VCOLS),
        in_specs=[
            tile(e), tile(e), tile(d),
            pl.BlockSpec((None, 3, d), lambda bi, i: (bi, 0, 0)),
            full3(dw_s), full3(dwb_s), full2(ln_g), full2(ln_b),
            pl.BlockSpec((None, e, d), lambda bi, i: (layer, 0, 0)),
            full2(bias), full2(final_g),
        ],
        out_specs=tile(d),
        out_shape=jax.ShapeDtypeStruct((b, n_grid_rows, GRID_W, d), F32),
        scratch_shapes=[
            pltpu.VMEM((n_slabs, n_grid_rows, VCOLS, CONV_LANES), F32),
            pltpu.VMEM((n_slabs, n_grid_rows * VCOLS, CONV_LANES), F32),
        ],
        compiler_params=_cparams("arbitrary", "arbitrary"),
        name="conv_cols_out_proj_final_norm",
    )(grid4(u), grid4(gate), grid4(x), modl, dw_s, dwb_s, ln_g, ln_b, w, bias, final_g)
    return out.reshape(b, l, d)


def _hgrn_layer(xa, modc, modl, norm_g, w_in, lb, hnorm_g, w_out, j, n_ctx, last_recurrent):
    e = w_out.shape[1]
    d = xa.shape[2]
    q, v, gate = _proj_call("hgrn_proj_qvg", xa, modc, modl, norm_g,
                            [(w_in, j, 0), (w_in, j, e), (w_in, j, 4 * e)], [],
                            _epi_qvgate, [BF16] * 3, e, n_ctx, tn=256)
    gf, kf, gb, kb = _proj_call("hgrn_proj_forget", xa, modc, modl, norm_g,
                                [(w_in, j, 2 * e), (w_in, j, 3 * e)],
                                [lb[0, j].reshape(1, e), lb[1, j].reshape(1, e)],
                                _epi_forget, [F32, BF16, F32, BF16], e, n_ctx, tn=256)
    og = _gla_call(q, v, gate, gf, kf, gb, kb, hnorm_g[j], n_ctx)
    return _out_call(og, xa, modc, modl, w_out, j, jnp.zeros((1, d), F32), n_ctx, skip_ctx=last_recurrent)


def _conv_proj(xa, modc, modl, norm_g, w_in, b_in, j, n_ctx):
    e = w_in.shape[2] // 3
    bias = b_in[j].reshape(1, 3 * e)
    return _proj_call("conv_proj", xa, modc, modl, norm_g,
                      [(w_in, j, 0), (w_in, j, e), (w_in, j, 2 * e)],
                      [bias[:, :e], bias[:, e:2 * e], bias[:, 2 * e:]],
                      _epi_conv, [BF16, BF16], e, n_ctx, tn=256)


def kernel(x, c, ctx, c_ctx, norm_g, ada_w, ada_b, hgrn_w_in, hgrn_lb_logits, hgrn_norm_g, hgrn_w_out,
           conv_w_in, conv_b_in, conv_dw, conv_dw_b, conv_ln_g, conv_ln_b, conv_w_out, conv_b_out,
           final_norm_g):
    b, l, d = x.shape
    n_ctx = ctx.shape[1]
    depth = norm_g.shape[0]
    assert depth == 4, "layer schedule below is written for the 4-layer hybrid"

    bp = -(-(b + 1) // 8) * 8
    cs = jnp.concatenate([c, c_ctx[None, :], jnp.zeros((bp - b - 1, d), F32)], axis=0)
    mod = _ada_call(cs, ada_w, ada_b).reshape(depth, bp, 3, d)
    lb = _lb_call(hgrn_lb_logits)
    xa = jnp.concatenate([ctx, x], axis=1)
    hgrn_w_in, hgrn_w_out = hgrn_w_in.astype(BF16), hgrn_w_out.astype(BF16)
    conv_w_in, conv_w_out = conv_w_in.astype(BF16), conv_w_out.astype(BF16)

    def mods(i):
        return mod[i, b:b + 1], mod[i, :b]

    modc, modl = mods(0)
    xa = _hgrn_layer(xa, modc, modl, norm_g[0], hgrn_w_in, lb, hgrn_norm_g, hgrn_w_out, 0, n_ctx, False)
    modc, modl = mods(1)
    u, gate = _conv_proj(xa, modc, modl, norm_g[1], conv_w_in, conv_b_in, 0, n_ctx)
    xa = _convh_call(u, gate, xa, modc, modl, conv_dw[0], conv_dw_b[0], conv_ln_g[0], conv_ln_b[0],
                     conv_w_out, 0, conv_b_out[0], n_ctx)
    modc, modl = mods(2)
    xl = _hgrn_layer(xa, modc, modl, norm_g[2], hgrn_w_in, lb, hgrn_norm_g, hgrn_w_out, 1, n_ctx, True)
    modc, modl = mods(3)
    u, gate = _conv_proj(xl, modc, modl, norm_g[3], conv_w_in, conv_b_in, 1, 0)
    return _convv_call(u, gate, xl, modl, conv_dw[1], conv_dw_b[1], conv_ln_g[1], conv_ln_b[1],
                       conv_w_out, 1, conv_b_out[1], final_norm_g)
```

```python
import functools

import jax
import jax.numpy as jnp
from jax import lax
from jax.experimental import pallas as pl
from jax.experimental.pallas import tpu as pltpu

F32 = jnp.float32
BF16 = jnp.bfloat16

EPS = 1e-6
GRID_W = 64
HEAD = 128
CONV_W = 31
CONV_PAD = CONV_W // 2
ROWS = 256
GLA_CHUNK = 64
GLA_SUB = 16
SUBLANES = 8
CONV_LANES = 256
VCOLS = 16
VMEM_LIMIT_BYTES = 56 * 1024 * 1024


def _cparams(*sem):
    return pltpu.CompilerParams(dimension_semantics=sem, vmem_limit_bytes=VMEM_LIMIT_BYTES)


def _sigmoid(x):
    return 1.0 / (1.0 + jnp.exp(-x))


def _silu(x):
    return x * _sigmoid(x)


def _ada_kernel(c_ref, w_ref, b_ref, o_ref):
    a = _silu(c_ref[...]).astype(BF16)
    o_ref[...] = jnp.dot(a, w_ref[...].astype(BF16), preferred_element_type=F32) + b_ref[...]


def _ada_call(cs, ada_w, ada_b):
    depth, d, n = ada_w.shape
    bp = cs.shape[0]
    tn = min(n, 1024)
    assert n % tn == 0
    return pl.pallas_call(
        _ada_kernel,
        grid=(depth, n // tn),
        in_specs=[
            pl.BlockSpec((bp, d), lambda l, j: (0, 0)),
            pl.BlockSpec((None, d, tn), lambda l, j: (l, 0, j)),
            pl.BlockSpec((None, 1, tn), lambda l, j: (l, 0, j)),
        ],
        out_specs=pl.BlockSpec((None, bp, tn), lambda l, j: (l, 0, j)),
        out_shape=jax.ShapeDtypeStruct((depth, bp, n), F32),
        compiler_params=_cparams("arbitrary", "arbitrary"),
        name="ada",
    )(cs, ada_w, ada_b.reshape(depth, 1, n))


def _lb_kernel(logit_ref, o_ref):
    n_a = logit_ref.shape[1]
    z = [logit_ref[:, i, :] for i in range(n_a)]
    m = functools.reduce(jnp.maximum, z)
    e = [jnp.exp(v - m) for v in z]
    tot = functools.reduce(lambda a, b: a + b, e)
    acc = jnp.zeros_like(tot)
    for i in range(n_a):
        if i > 0:
            acc = acc + e[i] / tot
        o_ref[:, i, :] = acc


def _lb_call(logits):
    return pl.pallas_call(
        _lb_kernel,
        out_shape=jax.ShapeDtypeStruct(logits.shape, F32),
        name="hgrn_lower_bounds",
    )(logits)


def _proj_kernel(*refs, n_w, n_aux, n_ctx_chunks, epilogue):
    x_ref, modc_ref, modl_ref, ng_ref = refs[:4]
    w_refs = refs[4:4 + n_w]
    aux_refs = refs[4 + n_w:4 + n_w + n_aux]
    out_refs = refs[4 + n_w + n_aux:-1]
    h_scr = refs[-1]
    n_chunks = x_ref.shape[0] // ROWS

    @pl.when(pl.program_id(1) == 0)
    def _():
        def norm_rows(mod_ref):
            def body(r, carry):
                rows = pl.ds(pl.multiple_of(r * ROWS, ROWS), ROWS)
                xv = x_ref[rows, :]
                y = xv * lax.rsqrt(jnp.mean(xv * xv, axis=-1, keepdims=True) + EPS) * ng_ref[...]
                h_scr[rows, :] = (y * (1.0 + mod_ref[1:2, :]) + mod_ref[0:1, :]).astype(BF16)
                return carry
            return body
        lax.fori_loop(0, n_ctx_chunks, norm_rows(modc_ref), 0)
        lax.fori_loop(n_ctx_chunks, n_chunks, norm_rows(modl_ref), 0)

    aux = [a[...] for a in aux_refs]

    def body(r, carry):
        rows = pl.ds(pl.multiple_of(r * ROWS, ROWS), ROWS)
        h = h_scr[rows, :]
        accs = [jnp.dot(h, w_ref[...], preferred_element_type=F32) for w_ref in w_refs]
        for o_ref, o in zip(out_refs, epilogue(accs, aux)):
            o_ref[rows, :] = o.astype(o_ref.dtype)
        return carry
    lax.fori_loop(0, n_chunks, body, 0, unroll=3)


def _proj_call(name, xa, modc, modl, norm_g, weights, auxs, epilogue, out_dtypes, n_out, n_ctx, tn):
    b, t, d = xa.shape
    assert t % ROWS == 0 and n_ctx % ROWS == 0 and n_out % tn == 0
    in_specs = [
        pl.BlockSpec((None, t, d), lambda bi, j: (bi, 0, 0)),
        pl.BlockSpec((None, 3, d), lambda bi, j: (0, 0, 0)),
        pl.BlockSpec((None, 3, d), lambda bi, j: (bi, 0, 0)),
        pl.BlockSpec((1, d), lambda bi, j: (0, 0)),
    ]
    args = [xa, modc, modl, norm_g.reshape(1, d)]
    for w, layer, col0 in weights:
        assert col0 % tn == 0
        in_specs.append(pl.BlockSpec((None, d, tn), functools.partial(
            lambda bi, j, layer, blk0: (layer, 0, blk0 + j), layer=layer, blk0=col0 // tn)))
        args.append(w)
    for a in auxs:
        in_specs.append(pl.BlockSpec((1, tn), lambda bi, j: (0, j)))
        args.append(a)
    kern = functools.partial(_proj_kernel, n_w=len(weights), n_aux=len(auxs),
                             n_ctx_chunks=n_ctx // ROWS, epilogue=epilogue)
    return pl.pallas_call(
        kern,
        grid=(b, n_out // tn),
        in_specs=in_specs,
        out_specs=[pl.BlockSpec((None, t, tn), lambda bi, j: (bi, 0, j)) for _ in out_dtypes],
        out_shape=[jax.ShapeDtypeStruct((b, t, n_out), dt) for dt in out_dtypes],
        scratch_shapes=[pltpu.VMEM((t, d), BF16)],
        compiler_params=_cparams("arbitrary", "arbitrary"),
        name=name,
    )(*args)


def _epi_qvgate(accs, aux):
    q, v, gate = accs
    return _silu(q), v, _silu(gate)


def _epi_forget(accs, aux):
    outs = []
    for z, lb in zip(accs, aux):
        f = lb + (1.0 - lb) * _sigmoid(z)
        outs += [jnp.log2(f), 1.0 - f]
    return outs


def _epi_conv(accs, aux):
    a, gl, gate = accs
    ba, bg, bgate = aux
    return (a + ba) * _sigmoid(gl + bg), _silu(gate + bgate)


def _cumsum_rows(g, reverse):
    n, width = g.shape
    row = lax.broadcasted_iota(jnp.int32, (SUBLANES, width), 0)
    groups = [g[j:j + SUBLANES, :] for j in range(0, n, SUBLANES)]
    sh = 1
    while sh < SUBLANES:
        if reverse:
            groups = [x + jnp.where(row < SUBLANES - sh, pltpu.roll(x, SUBLANES - sh, 0), 0.0) for x in groups]
        else:
            groups = [x + jnp.where(row >= sh, pltpu.roll(x, sh, 0), 0.0) for x in groups]
        sh *= 2
    out, carry = [], None
    for x in (reversed(groups) if reverse else groups):
        if carry is not None:
            x = x + carry
        carry = x[0:1, :] if reverse else x[SUBLANES - 1:SUBLANES, :]
        out.append(x)
    return jnp.concatenate(out[::-1] if reverse else out, axis=0)


_NT = (((1,), (1,)), ((), ()))


def _gla_local(q, kf, kb, g2f, g2b, masks):
    c, width = q.shape
    b_f = _cumsum_rows(g2f, reverse=False)
    b_b = _cumsum_rows(g2b, reverse=True)
    end_f, end_b = b_f[c - 1:c, :], b_b[0:1, :]
    both = lambda f, r: jnp.concatenate([f, r], axis=1)
    qi = both((q * jnp.exp2(b_f)).astype(BF16), (q * jnp.exp2(b_b)).astype(BF16))
    ke = both((kf * jnp.exp2(end_f - b_f)).astype(BF16), (kb * jnp.exp2(end_b - b_b)).astype(BF16))
    d = both(jnp.exp2(end_f), jnp.exp2(end_b))

    def scaled(x, b, m, is_query, ref_row):
        out = []
        for i in range(c // m):
            r = ref_row(i)
            if r is None:
                out.append(jnp.zeros((m, width), BF16))
            else:
                blk = slice(i * m, (i + 1) * m)
                e = b[blk, :] - b[r:r + 1, :] if is_query else b[r:r + 1, :] - b[blk, :]
                out.append((x[blk, :] * jnp.exp2(e)).astype(BF16))
        return jnp.concatenate(out, axis=0)

    def scores(qs, ks):
        return lax.dot_general(qs, ks, _NT, preferred_element_type=F32)

    m = GLA_SUB
    mid = lambda i: i * m + m // 2
    p = (jnp.where(masks[0], scores(scaled(q, b_f, m, True, mid), scaled(kf, b_f, m, False, mid)), 0.0)
         + jnp.where(masks[1], scores(scaled(q, b_b, m, True, mid), scaled(kb, b_b, m, False, mid)), 0.0))
    level = 2
    while m < c:
        odd, even = (lambda i: i % 2 == 1), (lambda i: i % 2 == 0)
        pick = lambda use, row: (lambda i: row(i) if use(i) else None)
        qs = both(scaled(q, b_f, m, True, pick(odd, lambda i: i * m - 1)),
                  scaled(q, b_b, m, True, pick(even, lambda i: (i + 1) * m)))
        ks = both(scaled(kf, b_f, m, False, pick(even, lambda i: (i + 1) * m - 1)),
                  scaled(kb, b_b, m, False, pick(odd, lambda i: i * m)))
        s = scores(qs, ks)
        p = p + s if 2 * m == c else jnp.where(masks[level], s, p)
        m *= 2
        level += 1
    return p, qi, ke, d


def _gla_masks(c):
    t_idx = lax.broadcasted_iota(jnp.int32, (c, c), 0)
    s_idx = lax.broadcasted_iota(jnp.int32, (c, c), 1)
    m = GLA_SUB
    shift = m.bit_length() - 1
    same = (t_idx >> shift) == (s_idx >> shift)
    masks = [jnp.logical_and(same, s_idx <= t_idx), jnp.logical_and(same, s_idx >= t_idx)]
    while 2 * m < c:
        masks.append(((t_idx >> shift) ^ (s_idx >> shift)) == 1)
        m *= 2
        shift += 1
    return masks


def _gla_kernel(q_ref, v_ref, gate_ref, gf_ref, kf_ref, gb_ref, kb_ref, ng_ref, o_ref,
                oi_scr, qi_scr, ke_scr, p_scr, u_scr, d_scr, st_scr, *, n_ctx):
    t = q_ref.shape[0]
    n_chunks = t // GLA_CHUNK
    n_ctx_chunks = n_ctx // GLA_CHUNK
    fwd, bwd = slice(0, HEAD), slice(HEAD, 2 * HEAD)

    def chunk_rows(n):
        return pl.ds(pl.multiple_of(n * GLA_CHUNK, GLA_CHUNK), GLA_CHUNK)

    masks = _gla_masks(GLA_CHUNK)

    def local(n, carry):
        rows = chunk_rows(n)
        p, qi, ke, d = _gla_local(q_ref[rows, :].astype(F32), kf_ref[rows, :].astype(F32),
                                  kb_ref[rows, :].astype(F32), gf_ref[rows, :], gb_ref[rows, :], masks)
        p_scr[n] = p.astype(BF16)
        qi_scr[rows, :] = qi
        ke_scr[rows, :] = ke
        d_scr[n] = d
        return carry
    lax.fori_loop(0, n_chunks, local, 0, unroll=4)

    def local_products(n, carry):
        rows = chunk_rows(n)
        v = v_ref[rows, :]
        oi_scr[rows, :] = jnp.dot(p_scr[n], v, preferred_element_type=F32)
        u_scr[n] = jnp.dot(v.astype(F32).T.astype(BF16), ke_scr[rows, :], preferred_element_type=F32)
        return carry
    lax.fori_loop(0, n_chunks, local_products, 0, unroll=6)

    def scan(lanes, first_chunk, step, count, st):
        def body(i, st):
            n = first_chunk + step * i
            st_scr[n, :, lanes] = st.astype(BF16)
            return st * d_scr[n, :, lanes] + u_scr[n, :, lanes]
        return lax.fori_loop(0, count, body, st, unroll=2)

    st0 = jnp.zeros((HEAD, HEAD), F32)
    scan(fwd, 0, 1, n_chunks, st0)
    st_b = scan(bwd, n_ctx_chunks - 1, -1, n_ctx_chunks, st0)
    scan(bwd, n_chunks - 1, -1, n_chunks - n_ctx_chunks, st_b)

    def readout(n, carry):
        rows = chunk_rows(n)
        o = oi_scr[rows, :] + lax.dot_general(qi_scr[rows, :], st_scr[n], _NT, preferred_element_type=F32)
        o = o * lax.rsqrt(jnp.mean(o * o, axis=-1, keepdims=True) + EPS) * ng_ref[...]
        o_ref[rows, :] = (o * gate_ref[rows, :].astype(F32)).astype(o_ref.dtype)
        return carry
    lax.fori_loop(0, n_chunks, readout, 0, unroll=12)


def _gla_call(q, v, gate, gf, kf, gb, kb, norm_g, n_ctx):
    b, t, e = q.shape
    assert t % GLA_CHUNK == 0 and n_ctx % GLA_CHUNK == 0
    n_chunks = t // GLA_CHUNK
    spec =pl.BlockSpec((None, t, HEAD), lambda bi, h: (bi, 0, h))
    return pl.pallas_call(
        functools.partial(_gla_kernel, n_ctx=n_ctx),
        grid=(b, e // HEAD),
        in_specs=[spec] * 7 + [pl.BlockSpec((1, HEAD), lambda bi, h: (0, 0))],
        out_specs=spec,
        out_shape=jax.ShapeDtypeStruct((b, t, e), BF16),
        scratch_shapes=[
            pltpu.VMEM((t, HEAD), F32),
            pltpu.VMEM((t, 2 * HEAD), BF16),
            pltpu.VMEM((t, 2 * HEAD), BF16),
            pltpu.VMEM((n_chunks, GLA_CHUNK, GLA_CHUNK), BF16),
            pltpu.VMEM((n_chunks, HEAD, 2 * HEAD), F32),
            pltpu.VMEM((n_chunks, 1, 2 * HEAD), F32),
            pltpu.VMEM((n_chunks, HEAD, 2 * HEAD), BF16),
        ],
        compiler_params=_cparams("arbitrary", "arbitrary"),
        name="hgrn_recurrence",
    )(q, v, gate, gf, kf, gb, kb, norm_g.reshape(1, HEAD))


def _residual(x, y, mod_ref, fin_ref, final_norm):
    out = x + mod_ref[2:3, :] * y
    if final_norm:
        out = out * lax.rsqrt(jnp.mean(out * out, axis=-1, keepdims=True) + EPS) * fin_ref[...]
    return out


def _out_kernel(og_ref, x_ref, modc_ref, modl_ref, w_ref, b_ref, o_ref, *, n_ctx_tiles):
    y = jnp.dot(og_ref[...], w_ref[...], preferred_element_type=F32) + b_ref[...]
    is_ctx = pl.program_id(1) < n_ctx_tiles

    @pl.when(is_ctx)
    def _():
        o_ref[...] = _residual(x_ref[...], y, modc_ref, None, False)

    @pl.when(jnp.logical_not(is_ctx))
    def _():
        o_ref[...] = _residual(x_ref[...], y, modl_ref, None, False)


def _out_call(og, xa, modc, modl, w, layer, bias, n_ctx, skip_ctx):
    b, t, e = og.shape
    d = xa.shape[2]
    n_ctx_tiles = n_ctx // ROWS
    off = n_ctx_tiles if skip_ctx else 0
    n_tiles = t // ROWS - off
    return pl.pallas_call(
        functools.partial(_out_kernel, n_ctx_tiles=0 if skip_ctx else n_ctx_tiles),
        grid=(b, n_tiles),
        in_specs=[
            pl.BlockSpec((None, ROWS, e), lambda bi, i: (bi, i + off, 0)),
            pl.BlockSpec((None, ROWS, d), lambda bi, i: (bi, i + off, 0)),
            pl.BlockSpec((None, 3, d), lambda bi, i: (0, 0, 0)),
            pl.BlockSpec((None, 3, d), lambda bi, i: (bi, 0, 0)),
            pl.BlockSpec((None, e, d), lambda bi, i: (layer, 0, 0)),
            pl.BlockSpec((1, d), lambda bi, i: (0, 0)),
        ],
        out_specs=pl.BlockSpec((None, ROWS, d), lambda bi, i: (bi, i, 0)),
        out_shape=jax.ShapeDtypeStruct((b, n_tiles * ROWS, d), F32),
        compiler_params=_cparams("arbitrary", "arbitrary"),
        name="hgrn_out_proj",
    )(og, xa, modc, modl, w, bias)


def _ln_gate_project(c_scr, gate_slab, lng_ref, lnb_ref, w_ref, n_slabs, rows):
    e = n_slabs * CONV_LANES
    s1 = jnp.zeros((rows, 1), F32)
    for cc in range(n_slabs):
        s1 = s1 + jnp.sum(c_scr[cc], axis=-1, keepdims=True)
    mean = s1 / e
    s2 = jnp.zeros((rows, 1), F32)
    for cc in range(n_slabs):
        dv = c_scr[cc] - mean
        s2 = s2 + jnp.sum(dv * dv, axis=-1, keepdims=True)
    rstd = lax.rsqrt(s2 / e + EPS)
    y = None
    for cc in range(n_slabs):
        lanes = slice(cc * CONV_LANES, (cc + 1) * CONV_LANES)
        u = (c_scr[cc] - mean) * rstd * lng_ref[:, lanes] + lnb_ref[:, lanes]
        u = (_silu(u) * gate_slab(lanes)).astype(BF16)
        part = jnp.dot(u, w_ref[lanes, :], preferred_element_type=F32)
        y = part if y is None else y + part
    return y


def _convh_kernel(u_ref, gate_ref, x_ref, modc_ref, modl_ref, dw_ref, dwb_ref, lng_ref, lnb_ref,
                  w_ref, b_ref, o_ref, pad_scr, shift_scr, c_scr, *, n_ctx_tiles):
    n_slabs = pad_scr.shape[0]
    n_rows = ROWS // GRID_W
    halo = 16
    is_ctx = pl.program_id(1) < n_ctx_tiles

    for cc in range(n_slabs):
        lanes = slice(cc * CONV_LANES, (cc + 1) * CONV_LANES)
        for r in range(n_rows):
            pad_scr[cc, r, halo:halo + GRID_W, :] = u_ref[r * GRID_W:(r + 1) * GRID_W, lanes].astype(F32)

    zeros = jnp.zeros((halo, CONV_LANES), F32)

    @pl.when(is_ctx)
    def _():
        for cc in range(n_slabs):
            lanes = slice(cc * CONV_LANES, (cc + 1) * CONV_LANES)
            for r in range(n_rows):
                lo, hi = r * GRID_W, (r + 1) * GRID_W
                pad_scr[cc, r, 0:halo, :] = zeros if r == 0 else u_ref[lo - halo:lo, lanes].astype(F32)
                pad_scr[cc, r, halo + GRID_W:, :] = (
                    zeros if r == n_rows - 1 else u_ref[hi:hi + halo, lanes].astype(F32))

    @pl.when(jnp.logical_not(is_ctx))
    def _():
        for cc in range(n_slabs):
            for r in range(n_rows):
                pad_scr[cc, r, 0:halo, :] = zeros
                pad_scr[cc, r, halo + GRID_W:, :] = zeros

    def conv_slab(cc, carry):
        w = dw_ref[cc]
        for r in range(n_rows):
            acc = jnp.zeros((GRID_W, CONV_LANES), F32)
            span = GRID_W + SUBLANES * ((CONV_W + halo - CONV_PAD) // SUBLANES - 1)
            for res in range(SUBLANES):
                shift_scr[res] = pad_scr[cc, r, res:res + span, :]
            for res in range(SUBLANES):
                for off in range(0, span - GRID_W + 1, SUBLANES):
                    k = off + res - (halo - CONV_PAD)
                    if 0 <= k < CONV_W:
                        acc = acc + shift_scr[res, off:off + GRID_W, :] * w[k:k + 1, :]
            c_scr[cc, r * GRID_W:(r + 1) * GRID_W, :] = acc + dwb_ref[cc]
        return carry
    lax.fori_loop(0, n_slabs, conv_slab, 0)

    y = _ln_gate_project(c_scr, lambda lanes: gate_ref[:, lanes].astype(F32), lng_ref, lnb_ref,
                         w_ref, n_slabs, ROWS) + b_ref[...]

    @pl.when(is_ctx)
    def _():
        o_ref[...] = _residual(x_ref[...], y, modc_ref, None, False)

    @pl.when(jnp.logical_not(is_ctx))
    def _():
        o_ref[...] = _residual(x_ref[...], y, modl_ref, None, False)


def _convv_kernel(u_ref, gate_ref, x_ref, modl_ref, dw_ref, dwb_ref, lng_ref, lnb_ref,
                  w_ref, b_ref, fin_ref, o_ref, us_scr, c_scr):
    n_slabs = c_scr.shape[0]
    n_grid_rows = u_ref.shape[0]
    tokens = n_grid_rows * VCOLS

    for cc in range(n_slabs):
        us_scr[cc] = u_ref[:, :, cc * CONV_LANES:(cc + 1) * CONV_LANES].astype(F32)

    def conv_slab(cc, carry):
        w = dw_ref[cc]
        for r in range(n_grid_rows):
            acc = jnp.zeros((VCOLS, CONV_LANES), F32)
            for k in range(CONV_W):
                src = r + k - CONV_PAD
                if 0 <= src < n_grid_rows:
                    acc = acc + us_scr[cc, src] * w[k:k + 1, :]
            c_scr[cc, r * VCOLS:(r + 1) * VCOLS, :] = acc + dwb_ref[cc]
        return carry
    lax.fori_loop(0, n_slabs, conv_slab, 0)

    def gate_slab(lanes):
        return gate_ref[:, :, lanes].astype(F32).reshape(tokens, CONV_LANES)

    y = _ln_gate_project(c_scr, gate_slab, lng_ref, lnb_ref, w_ref, n_slabs, tokens) + b_ref[...]
    d = x_ref.shape[-1]
    out = _residual(x_ref[...].reshape(tokens, d), y, modl_ref, fin_ref, True)
    o_ref[...] = out.reshape(n_grid_rows, VCOLS, d)


def _slab_params(dw, dw_b):
    e = dw.shape[1]
    n_slabs = e // CONV_LANES
    dw_s = dw.reshape(CONV_W, n_slabs, CONV_LANES).transpose(1, 0, 2)
    return dw_s, dw_b.reshape(n_slabs, 1, CONV_LANES), n_slabs


def _convh_call(u, gate, xa, modc, modl, dw, dw_b, ln_g, ln_b, w, layer, bias, n_ctx):
    b, t, e = u.shape
    d = xa.shape[2]
    assert n_ctx in (0, ROWS) and ROWS % GRID_W == 0
    dw_s, dwb_s, n_slabs = _slab_params(dw, dw_b)
    tile = lambda width: pl.BlockSpec((None, ROWS, width), lambda bi, i: (bi, i, 0))
    full2 = lambda a: pl.BlockSpec(a.shape, lambda bi, i: (0, 0))
    full3 = lambda a: pl.BlockSpec(a.shape, lambda bi, i: (0, 0, 0))
    ln_g, ln_b, bias = ln_g.reshape(1, e), ln_b.reshape(1, e), bias.reshape(1, d)
    return pl.pallas_call(
        functools.partial(_convh_kernel, n_ctx_tiles=n_ctx // ROWS),
        grid=(b, t // ROWS),
        in_specs=[
            tile(e), tile(e), tile(d),
            pl.BlockSpec((None, 3, d), lambda bi, i: (0, 0, 0)),
            pl.BlockSpec((None, 3, d), lambda bi, i: (bi, 0, 0)),
            full3(dw_s), full3(dwb_s), full2(ln_g), full2(ln_b),
            pl.BlockSpec((None, e, d), lambda bi, i: (layer, 0, 0)),
            full2(bias),
        ],
        out_specs=tile(d),
        out_shape=jax.ShapeDtypeStruct((b, t, d), F32),
        scratch_shapes=[
            pltpu.VMEM((n_slabs, ROWS // GRID_W, GRID_W + 32, CONV_LANES), F32),
            pltpu.VMEM((SUBLANES, GRID_W + 24, CONV_LANES), F32),
            pltpu.VMEM((n_slabs, ROWS, CONV_LANES), F32),
        ],
        compiler_params=_cparams("arbitrary", "arbitrary"),
        name="conv_rows_out_proj",
    )(u, gate, xa, modc, modl, dw_s, dwb_s, ln_g, ln_b, w, bias)


def _convv_call(u, gate, x, modl, dw, dw_b, ln_g, ln_b, w, layer, bias, final_g):
    b, l, e = u.shape
    d = x.shape[2]
    n_grid_rows = l // GRID_W
    dw_s, dwb_s, n_slabs = _slab_params(dw, dw_b)
    grid4 = lambda a: a.reshape(b, n_grid_rows, GRID_W, a.shape[-1])
    tile = lambda width: pl.BlockSpec((None, n_grid_rows, VCOLS, width), lambda bi, i: (bi, 0, i, 0))
    full2 = lambda a: pl.BlockSpec(a.shape, lambda bi, i: (0, 0))
    full3 = lambda a: pl.BlockSpec(a.shape, lambda bi, i: (0, 0, 0))
    ln_g, ln_b, bias, final_g = ln_g.reshape(1, e), ln_b.reshape(1, e), bias.reshape(1, d), final_g.reshape(1, d)
    out = pl.pallas_call(
        _convv_kernel,
        grid=(b, GRID_W // VCOLS),
        in_specs=[
            tile(e), tile(e), tile(d),
            pl.BlockSpec((None, 3, d), lambda bi, i: (bi, 0, 0)),
            full3(dw_s), full3(dwb_s), full2(ln_g), full2(ln_b),
            pl.BlockSpec((None, e, d), lambda bi, i: (layer, 0, 0)),
            full2(bias), full2(final_g),
        ],
        out_specs=tile(d),
        out_shape=jax.ShapeDtypeStruct((b, n_grid_rows, GRID_W, d), F32),
        scratch_shapes=[
            pltpu.VMEM((n_slabs, n_grid_rows, VCOLS, CONV_LANES), F32),
            pltpu.VMEM((n_slabs, n_grid_rows * VCOLS, CONV_LANES), F32),
        ],
        compiler_params=_cparams("arbitrary", "arbitrary"),
        name="conv_cols_out_proj_final_norm",
    )(grid4(u), grid4(gate), grid4(x), modl, dw_s, dwb_s, ln_g, ln_b, w, bias, final_g)
    return out.reshape(b, l, d)


def _hgrn_layer(xa, modc, modl, norm_g, w_in, lb, hnorm_g, w_out, j, n_ctx, last_recurrent):
    e = w_out.shape[1]
    d = xa.shape[2]
    q, v, gate = _proj_call("hgrn_proj_qvg", xa, modc, modl, norm_g,
                            [(w_in, j, 0), (w_in, j, e), (w_in, j, 4 * e)], [],
                            _epi_qvgate, [BF16] * 3, e, n_ctx, tn=256)
    gf, kf, gb, kb = _proj_call("hgrn_proj_forget", xa, modc, modl, norm_g,
                                [(w_in, j, 2 * e), (w_in, j, 3 * e)],
                                [lb[0, j].reshape(1, e), lb[1, j].reshape(1, e)],
                                _epi_forget, [F32, BF16, F32, BF16], e, n_ctx, tn=256)
    og = _gla_call(q, v, gate, gf, kf, gb, kb, hnorm_g[j], n_ctx)
    return _out_call(og, xa, modc, modl, w_out, j, jnp.zeros((1, d), F32), n_ctx, skip_ctx=last_recurrent)


def _conv_proj(xa, modc, modl, norm_g, w_in, b_in, j, n_ctx):
    e = w_in.shape[2] // 3
    bias = b_in[j].reshape(1, 3 * e)
    return _proj_call("conv_proj", xa, modc, modl, norm_g,
                      [(w_in, j, 0), (w_in, j, e), (w_in, j, 2 * e)],
                      [bias[:, :e], bias[:, e:2 * e], bias[:, 2 * e:]],
                      _epi_conv, [BF16, BF16], e, n_ctx, tn=256)


def kernel(x, c, ctx, c_ctx, norm_g, ada_w, ada_b, hgrn_w_in, hgrn_lb_logits, hgrn_norm_g, hgrn_w_out,
           conv_w_in, conv_b_in, conv_dw, conv_dw_b, conv_ln_g, conv_ln_b, conv_w_out, conv_b_out,
           final_norm_g):
    b, l, d = x.shape
    n_ctx = ctx.shape[1]
    depth = norm_g.shape[0]
    assert depth == 4, "layer schedule below is written for the 4-layer hybrid"

    bp = -(-(b + 1) // 8) * 8
    cs = jnp.concatenate([c, c_ctx[None, :], jnp.zeros((bp - b - 1, d), F32)], axis=0)
    mod = _ada_call(cs, ada_w, ada_b).reshape(depth, bp, 3, d)
    lb = _lb_call(hgrn_lb_logits)
    xa = jnp.concatenate([ctx, x], axis=1)
    hgrn_w_in, hgrn_w_out = hgrn_w_in.astype(BF16), hgrn_w_out.astype(BF16)
    conv_w_in, conv_w_out = conv_w_in.astype(BF16), conv_w_out.astype(BF16)

    def mods(i):
        return mod[i, b:b + 1], mod[i, :b]

    modc, modl = mods(0)
    xa = _hgrn_layer(xa, modc, modl, norm_g[0], hgrn_w_in, lb, hgrn_norm_g, hgrn_w_out, 0, n_ctx, False)
    modc, modl = mods(1)
    u, gate = _conv_proj(xa, modc, modl, norm_g[1], conv_w_in, conv_b_in, 0, n_ctx)
    xa = _convh_call(u, gate, xa, modc, modl, conv_dw[0], conv_dw_b[0], conv_ln_g[0], conv_ln_b[0],
                     conv_w_out, 0, conv_b_out[0], n_ctx)
    modc, modl = mods(2)
    xl = _hgrn_layer(xa, modc, modl, norm_g[2], hgrn_w_in, lb, hgrn_norm_g, hgrn_w_out, 1, n_ctx, True)
    modc, modl = mods(3)
    u, gate = _conv_proj(xl, modc, modl, norm_g[3], conv_w_in, conv_b_in, 1, 0)
    return _convv_call(u, gate, xl, modl, conv_dw[1], conv_dw_b[1], conv_ln_g[1], conv_ln_b[1],
                       conv_w_out, 1, conv_b_out[1], final_norm_g)
```

```python
import functools

import jax
import jax.numpy as jnp
from jax import lax
from jax.experimental import pallas as pl
from jax.experimental.pallas import tpu as pltpu

F32 = jnp.float32
BF16 = jnp.bfloat16

EPS = 1e-6
GRID_W = 64
HEAD = 128
CONV_W = 31
CONV_PAD = CONV_W // 2
ROWS = 256
GLA_CHUNK = 64
GLA_SUB = 16
GLA_GROUP = 4
SUBLANES = 8
CONV_LANES = 256
VCOLS = 16
VMEM_LIMIT_BYTES = 56 * 1024 * 1024


def _cparams(*sem):
    return pltpu.CompilerParams(dimension_semantics=sem, vmem_limit_bytes=VMEM_LIMIT_BYTES)


def _sigmoid(x):
    return 1.0 / (1.0 + jnp.exp(-x))


def _silu(x):
    return x * _sigmoid(x)


def _ada_kernel(c_ref, w_ref, b_ref, o_ref):
    a = _silu(c_ref[...]).astype(BF16)
    o_ref[...] = jnp.dot(a, w_ref[...].astype(BF16), preferred_element_type=F32) + b_ref[...]


def _ada_call(cs, ada_w, ada_b):
    depth, d, n = ada_w.shape
    bp = cs.shape[0]
    tn = min(n, 1024)
    assert n % tn == 0
    return pl.pallas_call(
        _ada_kernel,
        grid=(depth, n // tn),
        in_specs=[
            pl.BlockSpec((bp, d), lambda l, j: (0, 0)),
            pl.BlockSpec((None, d, tn), lambda l, j: (l, 0, j)),
            pl.BlockSpec((None, 1, tn), lambda l, j: (l, 0, j)),
        ],
        out_specs=pl.BlockSpec((None, bp, tn), lambda l, j: (l, 0, j)),
        out_shape=jax.ShapeDtypeStruct((depth, bp, n), F32),
        compiler_params=_cparams("arbitrary", "arbitrary"),
        name="ada",
    )(cs, ada_w, ada_b.reshape(depth, 1, n))


def _lb_kernel(logit_ref, o_ref):
    n_a = logit_ref.shape[1]
    z = [logit_ref[:, i, :] for i in range(n_a)]
    m = functools.reduce(jnp.maximum, z)
    e = [jnp.exp(v - m) for v in z]
    tot = functools.reduce(lambda a, b: a + b, e)
    acc = jnp.zeros_like(tot)
    for i in range(n_a):
        if i > 0:
            acc = acc + e[i] / tot
        o_ref[:, i, :] = acc


def _lb_call(logits):
    return pl.pallas_call(
        _lb_kernel,
        out_shape=jax.ShapeDtypeStruct(logits.shape, F32),
        name="hgrn_lower_bounds",
    )(logits)


def _proj_kernel(*refs, n_w, n_aux, n_ctx_chunks, epilogue):
    x_ref, modc_ref, modl_ref, ng_ref = refs[:4]
    w_refs = refs[4:4 + n_w]
    aux_refs = refs[4 + n_w:4 + n_w + n_aux]
    out_refs = refs[4 + n_w + n_aux:-1]
    h_scr = refs[-1]
    n_chunks = x_ref.shape[0] // ROWS

    @pl.when(pl.program_id(1) == 0)
    def _():
        def norm_rows(mod_ref):
            def body(r, carry):
                rows = pl.ds(pl.multiple_of(r * ROWS, ROWS), ROWS)
                xv = x_ref[rows, :]
                y = xv * lax.rsqrt(jnp.mean(xv * xv, axis=-1, keepdims=True) + EPS) * ng_ref[...]
                h_scr[rows, :] = (y * (1.0 + mod_ref[1:2, :]) + mod_ref[0:1, :]).astype(BF16)
                return carry
            return body
        lax.fori_loop(0, n_ctx_chunks, norm_rows(modc_ref), 0)
        lax.fori_loop(n_ctx_chunks, n_chunks, norm_rows(modl_ref), 0)

    aux = [a[...] for a in aux_refs]

    def body(r, carry):
        rows = pl.ds(pl.multiple_of(r * ROWS, ROWS), ROWS)
        h = h_scr[rows, :]
        accs = [jnp.dot(h, w_ref[...], preferred_element_type=F32) for w_ref in w_refs]
        for o_ref, o in zip(out_refs, epilogue(accs, aux)):
            o_ref[rows, :] = o.astype(o_ref.dtype)
        return carry
    lax.fori_loop(0, n_chunks, body, 0, unroll=3)


def _proj_call(name, xa, modc, modl, norm_g, weights, auxs, epilogue, out_dtypes, n_out, n_ctx, tn):
    b, t, d = xa.shape
    assert t % ROWS == 0 and n_ctx % ROWS == 0 and n_out % tn == 0
    in_specs = [
        pl.BlockSpec((None, t, d), lambda bi, j: (bi, 0, 0)),
        pl.BlockSpec((None, 3, d), lambda bi, j: (0, 0, 0)),
        pl.BlockSpec((None, 3, d), lambda bi, j: (bi, 0, 0)),
        pl.BlockSpec((1, d), lambda bi, j: (0, 0)),
    ]
    args = [xa, modc, modl, norm_g.reshape(1, d)]
    for w, layer, col0 in weights:
        assert col0 % tn == 0
        in_specs.append(pl.BlockSpec((None, d, tn), functools.partial(
            lambda bi, j, layer, blk0: (layer, 0, blk0 + j), layer=layer, blk0=col0 // tn)))
        args.append(w)
    for a in auxs:
        in_specs.append(pl.BlockSpec((1, tn), lambda bi, j: (0, j)))
        args.append(a)
    kern = functools.partial(_proj_kernel, n_w=len(weights), n_aux=len(auxs),
                             n_ctx_chunks=n_ctx // ROWS, epilogue=epilogue)
    return pl.pallas_call(
        kern,
        grid=(b, n_out // tn),
        in_specs=in_specs,
        out_specs=[pl.BlockSpec((None, t, tn), lambda bi, j: (bi, 0, j)) for _ in out_dtypes],
        out_shape=[jax.ShapeDtypeStruct((b, t, n_out), dt) for dt in out_dtypes],
        scratch_shapes=[pltpu.VMEM((t, d), BF16)],
        compiler_params=_cparams("arbitrary", "arbitrary"),
        name=name,
    )(*args)


def _epi_qvgate(accs, aux):
    q, v, gate = accs
    return [_silu(q), v, _silu(gate)]


def _epi_forget(accs, aux):
    outs = []
    for z, lb in zip(accs, aux):
        f = lb + (1.0 - lb) * _sigmoid(z)
        outs += [jnp.log2(f), 1.0 - f]
    return outs


def _epi_conv(accs, aux):
    a, gl, gate = accs
    ba, bg, bgate = aux
    return (a + ba) * _sigmoid(gl + bg), _silu(gate + bgate)


def _cumsum_rows(g, reverse):
    n, width = g.shape
    row = lax.broadcasted_iota(jnp.int32, (SUBLANES, width), 0)
    groups = [g[j:j + SUBLANES, :] for j in range(0, n, SUBLANES)]
    sh = 1
    while sh < SUBLANES:
        if reverse:
            groups = [x + jnp.where(row < SUBLANES - sh, pltpu.roll(x, SUBLANES - sh, 0), 0.0) for x in groups]
        else:
            groups = [x + jnp.where(row >= sh, pltpu.roll(x, sh, 0), 0.0) for x in groups]
        sh *= 2
    out, carry = [], None
    for x in (reversed(groups) if reverse else groups):
        if carry is not None:
            x = x + carry
        carry = x[0:1, :] if reverse else x[SUBLANES - 1:SUBLANES, :]
        out.append(x)
    return jnp.concatenate(out[::-1] if reverse else out, axis=0)


_NT = (((1,), (1,)), ((), ()))


def _gla_local(q, kf, kb, g2f, g2b, masks):
    c, width = q.shape
    b_f = _cumsum_rows(g2f, reverse=False)
    b_b = _cumsum_rows(g2b, reverse=True)
    end_f, end_b = b_f[c - 1:c, :], b_b[0:1, :]
    both = lambda f, r: jnp.concatenate([f, r], axis=1)
    qi = both((q * jnp.exp2(b_f)).astype(BF16), (q * jnp.exp2(b_b)).astype(BF16))
    ke = both((kf * jnp.exp2(end_f - b_f)).astype(BF16), (kb * jnp.exp2(end_b - b_b)).astype(BF16))
    d = both(jnp.exp2(end_f), jnp.exp2(end_b))

    def scaled(x, b, m, is_query, ref_row):
        out = []
        for i in range(c // m):
            r = ref_row(i)
            if r is None:
                out.append(jnp.zeros((m, width), BF16))
            else:
                blk = slice(i * m, (i + 1) * m)
                e = b[blk, :] - b[r:r + 1, :] if is_query else b[r:r + 1, :] - b[blk, :]
                out.append((x[blk, :] * jnp.exp2(e)).astype(BF16))
        return jnp.concatenate(out, axis=0)

    def scores(qs, ks):
        return lax.dot_general(qs, ks, _NT, preferred_element_type=F32)

    m = GLA_SUB
    mid = lambda i: i * m + m // 2
    p = (jnp.where(masks[0], scores(scaled(q, b_f, m, True, mid), scaled(kf, b_f, m, False, mid)), 0.0)
         + jnp.where(masks[1], scores(scaled(q, b_b, m, True, mid), scaled(kb, b_b, m, False, mid)), 0.0))
    level = 2
    while m < c:
        odd, even = (lambda i: i % 2 == 1), (lambda i: i % 2 == 0)
        pick = lambda use, row: (lambda i: row(i) if use(i) else None)
        qs = both(scaled(q, b_f, m, True, pick(odd, lambda i: i * m - 1)),
                  scaled(q, b_b, m, True, pick(even, lambda i: (i + 1) * m)))
        ks = both(scaled(kf, b_f, m, False, pick(even, lambda i: (i + 1) * m - 1)),
                  scaled(kb, b_b, m, False, pick(odd, lambda i: i * m)))
        s = scores(qs, ks)
        p = p + s if 2 * m == c else jnp.where(masks[level], s, p)
        m *= 2
        level += 1
    return p, qi, ke, d


def _gla_masks(c):
    t_idx = lax.broadcasted_iota(jnp.int32, (c, c), 0)
    s_idx = lax.broadcasted_iota(jnp.int32, (c, c), 1)
    m = GLA_SUB
    shift = m.bit_length() - 1
    same = (t_idx >> shift) == (s_idx >> shift)
    masks = [jnp.logical_and(same, s_idx <= t_idx), jnp.logical_and(same, s_idx >= t_idx)]
    while 2 * m < c:
        masks.append(((t_idx >> shift) ^ (s_idx >> shift)) == 1)
        m *= 2
        shift += 1
    return masks


def _gla_kernel(q_ref, v_ref, gate_ref, gf_ref, kf_ref, gb_ref, kb_ref, ng_ref, o_ref,
                oi_scr, qi_scr, ke_scr, p_scr, u_scr, d_scr, st_scr, *, n_ctx):
    t = q_ref.shape[0]
    n_chunks = t // GLA_CHUNK
    n_ctx_chunks = n_ctx // GLA_CHUNK
    fwd, bwd = slice(0, HEAD), slice(HEAD, 2 * HEAD)

    def chunk_rows(n):
        start = n * GLA_CHUNK
        return pl.ds(start if isinstance(n, int) else pl.multiple_of(start, GLA_CHUNK), GLA_CHUNK)

    masks = _gla_masks(GLA_CHUNK)

    def local(group):
        for n in (group * GLA_GROUP + j for j in range(GLA_GROUP)):
            rows = chunk_rows(n)
            p, qi, ke, d = _gla_local(q_ref[rows, :].astype(F32), kf_ref[rows, :].astype(F32),
                                      kb_ref[rows, :].astype(F32), gf_ref[rows, :], gb_ref[rows, :], masks)
            p_scr[n] = p.astype(BF16)
            qi_scr[rows, :] = qi
            ke_scr[rows, :] = ke
            d_scr[n] = d

    def local_products(group):
        for n in (group * GLA_GROUP + j for j in range(GLA_GROUP)):
            rows = chunk_rows(n)
            v = v_ref[rows, :]
            oi_scr[rows, :] = jnp.dot(p_scr[n], v, preferred_element_type=F32)
            u_scr[n] = jnp.dot(v.astype(F32).T.astype(BF16), ke_scr[rows, :], preferred_element_type=F32)

    def skewed(group, carry):
        local_products(group - 1)
        local(group)
        return carry

    n_groups = n_chunks // GLA_GROUP
    local(0)
    lax.fori_loop(1, n_groups, skewed, 0)
    local_products(n_groups - 1)

    def scan(lanes, first_chunk, step, count, st):
        def body(i, st):
            n = first_chunk + step * i
            st_scr[n, :, lanes] = st.astype(BF16)
            return st * d_scr[n, :, lanes] + u_scr[n, :, lanes]
        return lax.fori_loop(0, count, body, st, unroll=2)

    st0 = jnp.zeros((HEAD, HEAD), F32)
    scan(fwd, 0, 1, n_chunks, st0)
    st_b = scan(bwd, n_ctx_chunks - 1, -1, n_ctx_chunks, st0)
    scan(bwd, n_chunks - 1, -1, n_chunks - n_ctx_chunks, st_b)

    def readout(n, carry):
        rows = chunk_rows(n)
        o = oi_scr[rows, :] + lax.dot_general(qi_scr[rows, :], st_scr[n], _NT, preferred_element_type=F32)
        o = o * lax.rsqrt(jnp.mean(o * o, axis=-1, keepdims=True) + EPS) * ng_ref[...]
        o_ref[rows, :] = (o * gate_ref[rows, :].astype(F32)).astype(o_ref.dtype)
        return carry
    lax.fori_loop(0, n_chunks, readout, 0, unroll=12)


def _gla_call(q, v, gate, gf, kf, gb, kb, norm_g, n_ctx):
    b, t, e = q.shape
    assert t % GLA_CHUNK == 0 and n_ctx % GLA_CHUNK == 0
    n_chunks = t // GLA_CHUNK
    spec =pl.BlockSpec((None, t, HEAD), lambda bi, h: (bi, 0, h))
    return pl.pallas_call(
        functools.partial(_gla_kernel, n_ctx=n_ctx),
        grid=(b, e // HEAD),
        in_specs=[spec] * 7 + [pl.BlockSpec((1, HEAD), lambda bi, h: (0, 0))],
        out_specs=spec,
        out_shape=jax.ShapeDtypeStruct((b, t, e), BF16),
        scratch_shapes=[
            pltpu.VMEM((t, HEAD), F32),
            pltpu.VMEM((t, 2 * HEAD), BF16),
            pltpu.VMEM((t, 2 * HEAD), BF16),
            pltpu.VMEM((n_chunks, GLA_CHUNK, GLA_CHUNK), BF16),
            pltpu.VMEM((n_chunks, HEAD, 2 * HEAD), F32),
            pltpu.VMEM((n_chunks, 1, 2 * HEAD), F32),
            pltpu.VMEM((n_chunks, HEAD, 2 * HEAD), BF16),
        ],
        compiler_params=_cparams("arbitrary", "arbitrary"),
        name="hgrn_recurrence",
    )(q, v, gate, gf, kf, gb, kb, norm_g.reshape(1, HEAD))


def _residual(x, y, mod_ref, fin_ref, final_norm):
    out = x + mod_ref[2:3, :] * y
    if final_norm:
        out = out * lax.rsqrt(jnp.mean(out * out, axis=-1, keepdims=True) + EPS) * fin_ref[...]
    return out


def _out_kernel(og_ref, x_ref, modc_ref, modl_ref, w_ref, b_ref, o_ref, *, n_ctx_tiles):
    y = jnp.dot(og_ref[...], w_ref[...], preferred_element_type=F32) + b_ref[...]
    is_ctx = pl.program_id(1) < n_ctx_tiles

    @pl.when(is_ctx)
    def _():
        o_ref[...] = _residual(x_ref[...], y, modc_ref, None, False)

    @pl.when(jnp.logical_not(is_ctx))
    def _():
        o_ref[...] = _residual(x_ref[...], y, modl_ref, None, False)


def _out_call(og, xa, modc, modl, w, layer, bias, n_ctx, skip_ctx):
    b, t, e = og.shape
    d = xa.shape[2]
    n_ctx_tiles = n_ctx // ROWS
    off = n_ctx_tiles if skip_ctx else 0
    n_tiles = t // ROWS - off
    return pl.pallas_call(
        functools.partial(_out_kernel, n_ctx_tiles=0 if skip_ctx else n_ctx_tiles),
        grid=(b, n_tiles),
        in_specs=[
            pl.BlockSpec((None, ROWS, e), lambda bi, i: (bi, i + off, 0)),
            pl.BlockSpec((None, ROWS, d), lambda bi, i: (bi, i + off, 0)),
            pl.BlockSpec((None, 3, d), lambda bi, i: (0, 0, 0)),
            pl.BlockSpec((None, 3, d), lambda bi, i: (bi, 0, 0)),
            pl.BlockSpec((None, e, d), lambda bi, i: (layer, 0, 0)),
            pl.BlockSpec((1, d), lambda bi, i: (0, 0)),
        ],
        out_specs=pl.BlockSpec((None, ROWS, d), lambda bi, i: (bi, i, 0)),
        out_shape=jax.ShapeDtypeStruct((b, n_tiles * ROWS, d), F32),
        compiler_params=_cparams("arbitrary", "arbitrary"),
        name="hgrn_out_proj",
    )(og, xa, modc, modl, w, bias)


def _ln_gate_project(c_scr, gate_slab, lng_ref, lnb_ref, w_ref, n_slabs, rows):
    e = n_slabs * CONV_LANES
    s1 = jnp.zeros((rows, 1), F32)
    for cc in range(n_slabs):
        s1 = s1 + jnp.sum(c_scr[cc], axis=-1, keepdims=True)
    mean = s1 / e
    s2 = jnp.zeros((rows, 1), F32)
    for cc in range(n_slabs):
        dv = c_scr[cc] - mean
        s2 = s2 + jnp.sum(dv * dv, axis=-1, keepdims=True)
    rstd = lax.rsqrt(s2 / e + EPS)
    y = None
    for cc in range(n_slabs):
        lanes = slice(cc * CONV_LANES, (cc + 1) * CONV_LANES)
        u = (c_scr[cc] - mean) * rstd * lng_ref[:, lanes] + lnb_ref[:, lanes]
        u = (_silu(u) * gate_slab(lanes)).astype(BF16)
        part = jnp.dot(u, w_ref[lanes, :], preferred_element_type=F32)
        y = part if y is None else y + part
    return y


def _convh_kernel(u_ref, gate_ref, x_ref, modc_ref, modl_ref, dw_ref, dwb_ref, lng_ref, lnb_ref,
                  w_ref, b_ref, o_ref, pad_scr, shift_scr, c_scr, *, n_ctx_tiles):
    n_slabs = pad_scr.shape[0]
    n_rows = ROWS // GRID_W
    halo = 16
    is_ctx = pl.program_id(1) < n_ctx_tiles

    for cc in range(n_slabs):
        lanes = slice(cc * CONV_LANES, (cc + 1) * CONV_LANES)
        for r in range(n_rows):
            pad_scr[cc, r, halo:halo + GRID_W, :] = u_ref[r * GRID_W:(r + 1) * GRID_W, lanes].astype(F32)

    zeros = jnp.zeros((halo, CONV_LANES), F32)

    @pl.when(is_ctx)
    def _():
        for cc in range(n_slabs):
            lanes = slice(cc * CONV_LANES, (cc + 1) * CONV_LANES)
            for r in range(n_rows):
                lo, hi = r * GRID_W, (r + 1) * GRID_W
                pad_scr[cc, r, 0:halo, :] = zeros if r == 0 else u_ref[lo - halo:lo, lanes].astype(F32)
                pad_scr[cc, r, halo + GRID_W:, :] = (
                    zeros if r == n_rows - 1 else u_ref[hi:hi + halo, lanes].astype(F32))

    @pl.when(jnp.logical_not(is_ctx))
    def _():
        for cc in range(n_slabs):
            for r in range(n_rows):
                pad_scr[cc, r, 0:halo, :] = zeros
                pad_scr[cc, r, halo + GRID_W:, :] = zeros

    def conv_slab(cc, carry):
        w = dw_ref[cc]
        for r in range(n_rows):
            acc = jnp.zeros((GRID_W, CONV_LANES), F32)
            span = GRID_W + SUBLANES * ((CONV_W + halo - CONV_PAD) // SUBLANES - 1)
            for res in range(SUBLANES):
                shift_scr[res] = pad_scr[cc, r, res:res + span, :]
            for res in range(SUBLANES):
                for off in range(0, span - GRID_W + 1, SUBLANES):
                    k = off + res - (halo - CONV_PAD)
                    if 0 <= k < CONV_W:
                        acc = acc + shift_scr[res, off:off + GRID_W, :] * w[k:k + 1, :]
            c_scr[cc, r * GRID_W:(r + 1) * GRID_W, :] = acc + dwb_ref[cc]
        return carry
    lax.fori_loop(0, n_slabs, conv_slab, 0)

    y = _ln_gate_project(c_scr, lambda lanes: gate_ref[:, lanes].astype(F32), lng_ref, lnb_ref,
                         w_ref, n_slabs, ROWS) + b_ref[...]

    @pl.when(is_ctx)
    def _():
        o_ref[...] = _residual(x_ref[...], y, modc_ref, None, False)

    @pl.when(jnp.logical_not(is_ctx))
    def _():
        o_ref[...] = _residual(x_ref[...], y, modl_ref, None, False)


def _convv_kernel(u_ref, gate_ref, x_ref, modl_ref, dw_ref, dwb_ref, lng_ref, lnb_ref,
                  w_ref, b_ref, fin_ref, o_ref, us_scr, c_scr):
    n_slabs = c_scr.shape[0]
    n_grid_rows = u_ref.shape[0]
    tokens = n_grid_rows * VCOLS

    for cc in range(n_slabs):
        us_scr[cc] = u_ref[:, :, cc * CONV_LANES:(cc + 1) * CONV_LANES].astype(F32)

    def conv_slab(cc, carry):
        w = dw_ref[cc]
        for r in range(n_grid_rows):
            acc = jnp.zeros((VCOLS, CONV_LANES), F32)
            for k in range(CONV_W):
                src = r + k - CONV_PAD
                if 0 <= src < n_grid_rows:
                    acc = acc + us_scr[cc, src] * w[k:k + 1, :]
            c_scr[cc, r * VCOLS:(r + 1) * VCOLS, :] = acc + dwb_ref[cc]
        return carry
    lax.fori_loop(0, n_slabs, conv_slab, 0)

    def gate_slab(lanes):
        return gate_ref[:, :, lanes].astype(F32).reshape(tokens, CONV_LANES)

    y = _ln_gate_project(c_scr, gate_slab, lng_ref, lnb_ref, w_ref, n_slabs, tokens) + b_ref[...]
    d = x_ref.shape[-1]
    out = _residual(x_ref[...].reshape(tokens, d), y, modl_ref, fin_ref, True)
    o_ref[...] = out.reshape(n_grid_rows, VCOLS, d)


def _slab_params(dw, dw_b):
    e = dw.shape[1]
    n_slabs = e // CONV_LANES
    dw_s = dw.reshape(CONV_W, n_slabs, CONV_LANES).transpose(1, 0, 2)
    return dw_s, dw_b.reshape(n_slabs, 1, CONV_LANES), n_slabs


def _convh_call(u, gate, xa, modc, modl, dw, dw_b, ln_g, ln_b, w, layer, bias, n_ctx):
    b, t, e = u.shape
    d = xa.shape[2]
    assert n_ctx in (0, ROWS) and ROWS % GRID_W == 0
    dw_s, dwb_s, n_slabs = _slab_params(dw, dw_b)
    tile = lambda width: pl.BlockSpec((None, ROWS, width), lambda bi, i: (bi, i, 0))
    full2 = lambda a: pl.BlockSpec(a.shape, lambda bi, i: (0, 0))
    full3 = lambda a: pl.BlockSpec(a.shape, lambda bi, i: (0, 0, 0))
    ln_g, ln_b, bias = ln_g.reshape(1, e), ln_b.reshape(1, e), bias.reshape(1, d)
    return pl.pallas_call(
        functools.partial(_convh_kernel, n_ctx_tiles=n_ctx // ROWS),
        grid=(b, t // ROWS),
        in_specs=[
            tile(e), tile(e), tile(d),
            pl.BlockSpec((None, 3, d), lambda bi, i: (0, 0, 0)),
            pl.BlockSpec((None, 3, d), lambda bi, i: (bi, 0, 0)),
            full3(dw_s), full3(dwb_s), full2(ln_g), full2(ln_b),
            pl.BlockSpec((None, e, d), lambda bi, i: (layer, 0, 0)),
            full2(bias),
        ],
        out_specs=tile(d),
        out_shape=jax.ShapeDtypeStruct((b, t, d), F32),
        scratch_shapes=[
            pltpu.VMEM((n_slabs, ROWS // GRID_W, GRID_W + 32, CONV_LANES), F32),
            pltpu.VMEM((SUBLANES, GRID_W + 24, CONV_LANES), F32),
            pltpu.VMEM((n_slabs, ROWS, CONV_LANES), F32),
        ],
        compiler_params=_cparams("arbitrary", "arbitrary"),
        name="conv_rows_out_proj",
    )(u, gate, xa, modc, modl, dw_s, dwb_s, ln_g, ln_b, w, bias)


def _convv_call(u, gate, x, modl, dw, dw_b, ln_g, ln_b, w, layer, bias, final_g):
    b, l, e = u.shape
    d = x.shape[2]
    n_grid_rows = l // GRID_W
    dw_s, dwb_s, n_slabs = _slab_params(dw, dw_b)
    grid4 = lambda a: a.reshape(b, n_grid_rows, GRID_W, a.shape[-1])
    tile = lambda width: pl.BlockSpec((None, n_grid_rows, VCOLS, width), lambda bi, i: (bi, 0, i, 0))
    full2 = lambda a: pl.BlockSpec(a.shape, lambda bi, i: (0, 0))
    full3 = lambda a: pl.BlockSpec(a.shape, lambda bi, i: (0, 0, 0))
    ln_g, ln_b, bias, final_g = ln_g.reshape(1, e), ln_b.reshape(1, e), bias.reshape(1, d), final_g.reshape(1, d)
    out = pl.pallas_call(
        _convv_kernel,
        grid=(b, GRID_W // VCOLS),
        in_specs=[
            tile(e), tile(e), tile(d),
            pl.BlockSpec((None, 3, d), lambda bi, i: (bi, 0, 0)),
            full3(dw_s), full3(dwb_s), full2(ln_g), full2(ln_b),
            pl.BlockSpec((None, e, d), lambda bi, i: (layer, 0, 0)),
            full2(bias), full2(final_g),
        ],
        out_specs=tile(d),
        out_shape=jax.ShapeDtypeStruct((b, n_grid_rows, GRID_W, d), F32),
        scratch_shapes=[
            pltpu.VMEM((n_slabs, n_grid_rows, VCOLS, CONV_LANES), F32),
            pltpu.VMEM((n_slabs, n_grid_rows * VCOLS, CONV_LANES), F32),
        ],
        compiler_params=_cparams("arbitrary", "arbitrary"),
        name="conv_cols_out_proj_final_norm",
    )(grid4(u), grid4(gate), grid4(x), modl, dw_s, dwb_s, ln_g, ln_b, w, bias, final_g)
    return out.reshape(b, l, d)


def _hgrn_layer(xa, modc, modl, norm_g, w_in, lb, hnorm_g, w_out, j, n_ctx, last_recurrent):
    e = w_out.shape[1]
    d = xa.shape[2]
    q, v, gate, gf, kf, gb, kb = _proj_call(
        "hgrn_proj", xa, modc, modl, norm_g, [(w_in, j, i * e) for i in (0, 1, 4, 2, 3)],
        [lb[0, j].reshape(1, e), lb[1, j].reshape(1, e)],
        lambda accs, aux: _epi_qvgate(accs[:3], aux) + _epi_forget(accs[3:], aux),
        [BF16] * 3 + [F32, BF16, F32, BF16], e, n_ctx, tn=256)
    og = _gla_call(q, v, gate, gf, kf, gb, kb, hnorm_g[j], n_ctx)
    return _out_call(og, xa, modc, modl, w_out, j, jnp.zeros((1, d), F32), n_ctx, skip_ctx=last_recurrent)


def _conv_proj(xa, modc, modl, norm_g, w_in, b_in, j, n_ctx):
    e = w_in.shape[2] // 3
    bias = b_in[j].reshape(1, 3 * e)
    return _proj_call("conv_proj", xa, modc, modl, norm_g,
                      [(w_in, j, 0), (w_in, j, e), (w_in, j, 2 * e)],
                      [bias[:, :e], bias[:, e:2 * e], bias[:, 2 * e:]],
                      _epi_conv, [BF16, BF16], e, n_ctx, tn=256)


def kernel(x, c, ctx, c_ctx, norm_g, ada_w, ada_b, hgrn_w_in, hgrn_lb_logits, hgrn_norm_g, hgrn_w_out,
           conv_w_in, conv_b_in, conv_dw, conv_dw_b, conv_ln_g, conv_ln_b, conv_w_out, conv_b_out,
           final_norm_g):
    b, l, d = x.shape
    n_ctx = ctx.shape[1]
    depth = norm_g.shape[0]
    assert depth == 4, "layer schedule below is written for the 4-layer hybrid"

    bp = -(-(b + 1) // 8) * 8
    cs = jnp.concatenate([c, c_ctx[None, :], jnp.zeros((bp - b - 1, d), F32)], axis=0)
    mod = _ada_call(cs, ada_w, ada_b).reshape(depth, bp, 3, d)
    lb = _lb_call(hgrn_lb_logits)
    xa = jnp.concatenate([ctx, x], axis=1)
    hgrn_w_in, hgrn_w_out = hgrn_w_in.astype(BF16), hgrn_w_out.astype(BF16)
    conv_w_in, conv_w_out = conv_w_in.astype(BF16), conv_w_out.astype(BF16)

    def mods(i):
        return mod[i, b:b + 1], mod[i, :b]

    modc, modl = mods(0)
    xa = _hgrn_layer(xa, modc, modl, norm_g[0], hgrn_w_in, lb, hgrn_norm_g, hgrn_w_out, 0, n_ctx, False)
    modc, modl = mods(1)
    u, gate = _conv_proj(xa, modc, modl, norm_g[1], conv_w_in, conv_b_in, 0, n_ctx)
    xa = _convh_call(u, gate, xa, modc, modl, conv_dw[0], conv_dw_b[0], conv_ln_g[0], conv_ln_b[0],
                     conv_w_out, 0, conv_b_out[0], n_ctx)
    modc, modl = mods(2)
    xl = _hgrn_layer(xa, modc, modl, norm_g[2], hgrn_w_in, lb, hgrn_norm_g, hgrn_w_out, 1, n_ctx, True)
    modc, modl = mods(3)
    u, gate = _conv_proj(xl, modc, modl, norm_g[3], conv_w_in, conv_b_in, 1, 0)
    return _convv_call(u, gate, xl, modl, conv_dw[1], conv_dw_b[1], conv_ln_g[1], conv_ln_b[1],
                       conv_w_out, 1, conv_b_out[1], final_norm_g)
```

```python
import functools

import jax
import jax.numpy as jnp
from jax import lax
from jax.experimental import pallas as pl
from jax.experimental.pallas import tpu as pltpu

F32 = jnp.float32
BF16 = jnp.bfloat16

EPS = 1e-6
SUBLANES = 8
GRID_W = 64
HEAD = 128
CONV_W = 31
CONV_PAD = CONV_W // 2
CONV_HALO = 16
CONV_SPAN = GRID_W + SUBLANES * ((CONV_W + CONV_HALO - CONV_PAD) // SUBLANES - 1)
ROWS = 256
GLA_CHUNK = 64
GLA_SUB = 16
GLA_GROUP = 6
CONV_LANES = 256
VCOLS = 16
VMEM_LIMIT_BYTES = 56 * 1024 * 1024


def _cparams(*sem):
    return pltpu.CompilerParams(dimension_semantics=sem, vmem_limit_bytes=VMEM_LIMIT_BYTES)


def _sigmoid(x):
    return 1.0 / (1.0 + jnp.exp(-x))


def _silu(x):
    return x * _sigmoid(x)


def _ada_kernel(c_ref, w_ref, b_ref, o_ref):
    a = _silu(c_ref[...]).astype(BF16)
    o_ref[...] = jnp.dot(a, w_ref[...].astype(BF16), preferred_element_type=F32) + b_ref[...]


def _ada_call(cs, ada_w, ada_b):
    depth, d, n = ada_w.shape
    bp = cs.shape[0]
    tn = min(n, 1024)
    assert n % tn == 0
    return pl.pallas_call(
        _ada_kernel,
        grid=(depth, n // tn),
        in_specs=[
            pl.BlockSpec((bp, d), lambda l, j: (0, 0)),
            pl.BlockSpec((None, d, tn), lambda l, j: (l, 0, j)),
            pl.BlockSpec((None, 1, tn), lambda l, j: (l, 0, j)),
        ],
        out_specs=pl.BlockSpec((None, bp, tn), lambda l, j: (l, 0, j)),
        out_shape=jax.ShapeDtypeStruct((depth, bp, n), F32),
        compiler_params=_cparams("arbitrary", "arbitrary"),
        name="ada",
    )(cs, ada_w, ada_b.reshape(depth, 1, n))


def _lb_kernel(logit_ref, o_ref):
    n_a = logit_ref.shape[1]
    z = [logit_ref[:, i, :] for i in range(n_a)]
    m = functools.reduce(jnp.maximum, z)
    e = [jnp.exp(v - m) for v in z]
    tot = functools.reduce(lambda a, b: a + b, e)
    acc = jnp.zeros_like(tot)
    for i in range(n_a):
        if i > 0:
            acc = acc + e[i] / tot
        o_ref[:, i, :] = acc


def _lb_call(logits):
    return pl.pallas_call(
        _lb_kernel,
        out_shape=jax.ShapeDtypeStruct(logits.shape, F32),
        name="hgrn_lower_bounds",
    )(logits)


def _proj_kernel(*refs, n_w, n_aux, n_ctx_chunks, epilogue):
    x_ref, modc_ref, modl_ref, ng_ref = refs[:4]
    w_refs = refs[4:4 + n_w]
    aux_refs = refs[4 + n_w:4 + n_w + n_aux]
    out_refs = refs[4 + n_w + n_aux:-1]
    h_scr = refs[-1]
    n_chunks = x_ref.shape[0] // ROWS

    @pl.when(pl.program_id(1) == 0)
    def _():
        def norm_rows(mod_ref):
            gain = ng_ref[...] * (1.0 + mod_ref[1:2, :])

            def body(r, carry):
                rows = pl.ds(pl.multiple_of(r * ROWS, ROWS), ROWS)
                xv = x_ref[rows, :]
                y = xv * lax.rsqrt(jnp.mean(xv * xv, axis=-1, keepdims=True) + EPS)
                h_scr[rows, :] = (y * gain + mod_ref[0:1, :]).astype(BF16)
                return carry
            return body
        lax.fori_loop(0, n_ctx_chunks, norm_rows(modc_ref), 0)
        lax.fori_loop(n_ctx_chunks, n_chunks, norm_rows(modl_ref), 0)

    aux = [a[...] for a in aux_refs]

    def body(r, carry):
        rows = pl.ds(pl.multiple_of(r * ROWS, ROWS), ROWS)
        h = h_scr[rows, :]
        accs = [jnp.dot(h, w_ref[...], preferred_element_type=F32) for w_ref in w_refs]
        for o_ref, o in zip(out_refs, epilogue(accs, aux)):
            o_ref[rows, :] = o.astype(o_ref.dtype)
        return carry
    lax.fori_loop(0, n_chunks, body, 0, unroll=3)


def _proj_call(name, xa, modc, modl, norm_g, weights, auxs, epilogue, out_dtypes, n_out, n_ctx, tn):
    b, t, d = xa.shape
    assert t % ROWS == 0 and n_ctx % ROWS == 0 and n_out % tn == 0
    in_specs = [
        pl.BlockSpec((None, t, d), lambda bi, j: (bi, 0, 0)),
        pl.BlockSpec((None, 3, d), lambda bi, j: (0, 0, 0)),
        pl.BlockSpec((None, 3, d), lambda bi, j: (bi, 0, 0)),
        pl.BlockSpec((1, d), lambda bi, j: (0, 0)),
    ]
    args = [xa, modc, modl, norm_g.reshape(1, d)]
    for w, layer, col0 in weights:
        assert col0 % tn == 0
        in_specs.append(pl.BlockSpec((None, d, tn), functools.partial(
            lambda bi, j, layer, blk0: (layer, 0, blk0 + j), layer=layer, blk0=col0 // tn)))
        args.append(w)
    for a in auxs:
        in_specs.append(pl.BlockSpec((1, tn), lambda bi, j: (0, j)))
        args.append(a)
    kern = functools.partial(_proj_kernel, n_w=len(weights), n_aux=len(auxs),
                             n_ctx_chunks=n_ctx // ROWS, epilogue=epilogue)
    return pl.pallas_call(
        kern,
        grid=(b, n_out // tn),
        in_specs=in_specs,
        out_specs=[pl.BlockSpec((None, t, tn), lambda bi, j: (bi, 0, j)) for _ in out_dtypes],
        out_shape=[jax.ShapeDtypeStruct((b, t, n_out), dt) for dt in out_dtypes],
        scratch_shapes=[pltpu.VMEM((t, d), BF16)],
        compiler_params=_cparams("arbitrary", "arbitrary"),
        name=name,
    )(*args)


def _epi_qvgate(accs, aux):
    q, v, gate = accs
    return [_silu(q), v, _silu(gate)]


def _epi_forget(accs, aux):
    outs = []
    for z, lb in zip(accs, aux):
        f = lb + (1.0 - lb) * _sigmoid(z)
        outs += [jnp.log2(f), 1.0 - f]
    return outs


def _epi_conv(accs, aux):
    a, gl, gate = accs
    ba, bg, bgate = aux
    return (a + ba) * _sigmoid(gl + bg), _silu(gate + bgate)


def _cumsum_rows(g, reverse):
    n, width = g.shape
    row = lax.broadcasted_iota(jnp.int32, (SUBLANES, width), 0)
    groups = [g[j:j + SUBLANES, :] for j in range(0, n, SUBLANES)]
    sh = 1
    while sh < SUBLANES:
        if reverse:
            groups = [x + jnp.where(row < SUBLANES - sh, pltpu.roll(x, SUBLANES - sh, 0), 0.0) for x in groups]
        else:
            groups = [x + jnp.where(row >= sh, pltpu.roll(x, sh, 0), 0.0) for x in groups]
        sh *= 2
    out, carry = [], None
    for x in (reversed(groups) if reverse else groups):
        if carry is not None:
            x = x + carry
        carry = x[0:1, :] if reverse else x[SUBLANES - 1:SUBLANES, :]
        out.append(x)
    return jnp.concatenate(out[::-1] if reverse else out, axis=0)


_NT = (((1,), (1,)), ((), ()))


def _gla_local(q, kf, kb, g2f, g2b, masks):
    c, width = q.shape
    b_f = _cumsum_rows(g2f, reverse=False)
    b_b = _cumsum_rows(g2b, reverse=True)
    end_f, end_b = b_f[c - 1:c, :], b_b[0:1, :]
    both = lambda f, r: jnp.concatenate([f, r], axis=1)
    qi = both((q * jnp.exp2(b_f)).astype(BF16), (q * jnp.exp2(b_b)).astype(BF16))
    ke = both((kf * jnp.exp2(end_f - b_f)).astype(BF16), (kb * jnp.exp2(end_b - b_b)).astype(BF16))
    d = both(jnp.exp2(end_f), jnp.exp2(end_b))

    def scaled(x, b, m, is_query, ref_row):
        out = []
        for i in range(c // m):
            r = ref_row(i)
            if r is None:
                out.append(jnp.zeros((m, width), BF16))
            else:
                blk = slice(i * m, (i + 1) * m)
                e = b[blk, :] - b[r:r + 1, :] if is_query else b[r:r + 1, :] - b[blk, :]
                out.append((x[blk, :] * jnp.exp2(e)).astype(BF16))
        return jnp.concatenate(out, axis=0)

    def scores(qs, ks):
        return lax.dot_general(qs, ks, _NT, preferred_element_type=F32)

    m = GLA_SUB
    mid = lambda i: i * m + m // 2
    p = (jnp.where(masks[0], scores(scaled(q, b_f, m, True, mid), scaled(kf, b_f, m, False, mid)), 0.0)
         + jnp.where(masks[1], scores(scaled(q, b_b, m, True, mid), scaled(kb, b_b, m, False, mid)), 0.0))
    level = 2
    while m < c:
        odd, even = (lambda i: i % 2 == 1), (lambda i: i % 2 == 0)
        pick = lambda use, row: (lambda i: row(i) if use(i) else None)
        qs = both(scaled(q, b_f, m, True, pick(odd, lambda i: i * m - 1)),
                  scaled(q, b_b, m, True, pick(even, lambda i: (i + 1) * m)))
        ks = both(scaled(kf, b_f, m, False, pick(even, lambda i: (i + 1) * m - 1)),
                  scaled(kb, b_b, m, False, pick(odd, lambda i: i * m)))
        s = scores(qs, ks)
        p = p + s if 2 * m == c else jnp.where(masks[level], s, p)
        m *= 2
        level += 1
    return p, qi, ke, d


def _gla_masks(c):
    t_idx = lax.broadcasted_iota(jnp.int32, (c, c), 0)
    s_idx = lax.broadcasted_iota(jnp.int32, (c, c), 1)
    m = GLA_SUB
    shift = m.bit_length() - 1
    same = (t_idx >> shift) == (s_idx >> shift)
    masks = [jnp.logical_and(same, s_idx <= t_idx), jnp.logical_and(same, s_idx >= t_idx)]
    while 2 * m < c:
        masks.append(((t_idx >> shift) ^ (s_idx >> shift)) == 1)
        m *= 2
        shift += 1
    return masks


def _gla_kernel(q_ref, v_ref, gate_ref, gf_ref, kf_ref, gb_ref, kb_ref, ng_ref, o_ref,
                oi_scr, qi_scr, ke_scr, p_scr, u_scr, d_scr, st_scr, *, n_ctx):
    t = q_ref.shape[0]
    n_chunks = t // GLA_CHUNK
    n_ctx_chunks = n_ctx // GLA_CHUNK
    n_groups = n_chunks // GLA_GROUP
    fwd, bwd = slice(0, HEAD), slice(HEAD, 2 * HEAD)

    def chunk_rows(n):
        start = n * GLA_CHUNK
        return pl.ds(start if isinstance(n, int) else pl.multiple_of(start, GLA_CHUNK), GLA_CHUNK)

    def group_chunks(group):
        return (group * GLA_GROUP + j for j in range(GLA_GROUP))

    masks = _gla_masks(GLA_CHUNK)

    def local(group):
        for n in group_chunks(group):
            rows = chunk_rows(n)
            p, qi, ke, d = _gla_local(q_ref[rows, :].astype(F32), kf_ref[rows, :].astype(F32),
                                      kb_ref[rows, :].astype(F32), gf_ref[rows, :], gb_ref[rows, :], masks)
            p_scr[n] = p.astype(BF16)
            qi_scr[rows, :] = qi
            ke_scr[rows, :] = ke
            d_scr[n] = d

    def local_products(group):
        for n in group_chunks(group):
            rows = chunk_rows(n)
            v = v_ref[rows, :]
            oi_scr[rows, :] = jnp.dot(p_scr[n], v, preferred_element_type=F32)
            u_scr[n] = jnp.dot(v.astype(F32).T.astype(BF16), ke_scr[rows, :], preferred_element_type=F32)

    def skewed(group, carry):
        local_products(group - 1)
        local(group)
        return carry

    local(0)
    lax.fori_loop(1, n_groups, skewed, 0)
    local_products(n_groups - 1)

    def scan(lanes, first_chunk, step, count, st):
        def body(i, st):
            n = first_chunk + step * i
            st_scr[n, :, lanes] = st.astype(BF16)
            return st * d_scr[n, :, lanes] + u_scr[n, :, lanes]
        return lax.fori_loop(0, count, body, st, unroll=2)

    st0 = jnp.zeros((HEAD, HEAD), F32)
    scan(fwd, 0, 1, n_chunks, st0)
    st_b = scan(bwd, n_ctx_chunks - 1, -1, n_ctx_chunks, st0)
    scan(bwd, n_chunks - 1, -1, n_chunks - n_ctx_chunks, st_b)

    def readout(n, carry):
        rows = chunk_rows(n)
        o = oi_scr[rows, :] + lax.dot_general(qi_scr[rows, :], st_scr[n], _NT, preferred_element_type=F32)
        o = o * lax.rsqrt(jnp.mean(o * o, axis=-1, keepdims=True) + EPS) * ng_ref[...]
        o_ref[rows, :] = (o * gate_ref[rows, :].astype(F32)).astype(o_ref.dtype)
        return carry
    lax.fori_loop(0, n_chunks, readout, 0, unroll=12)


def _gla_call(q, v, gate, gf, kf, gb, kb, norm_g, n_ctx):
    b, t, e = q.shape
    assert t % GLA_CHUNK == 0 and n_ctx % GLA_CHUNK == 0 and (t // GLA_CHUNK) % GLA_GROUP == 0
    n_chunks = t // GLA_CHUNK
    spec = pl.BlockSpec((None, t, HEAD), lambda bi, h: (bi, 0, h))
    return pl.pallas_call(
        functools.partial(_gla_kernel, n_ctx=n_ctx),
        grid=(b, e // HEAD),
        in_specs=[spec] * 7 + [pl.BlockSpec((1, HEAD), lambda bi, h: (0, 0))],
        out_specs=spec,
        out_shape=jax.ShapeDtypeStruct((b, t, e), BF16),
        scratch_shapes=[
            pltpu.VMEM((t, HEAD), F32),
            pltpu.VMEM((t, 2 * HEAD), BF16),
            pltpu.VMEM((t, 2 * HEAD), BF16),
            pltpu.VMEM((n_chunks, GLA_CHUNK, GLA_CHUNK), BF16),
            pltpu.VMEM((n_chunks, HEAD, 2 * HEAD), F32),
            pltpu.VMEM((n_chunks, 1, 2 * HEAD), F32),
            pltpu.VMEM((n_chunks, HEAD, 2 * HEAD), BF16),
        ],
        compiler_params=_cparams("arbitrary", "arbitrary"),
        name="hgrn_recurrence",
    )(q, v, gate, gf, kf, gb, kb, norm_g.reshape(1, HEAD))


def _residual(x, y, mod_ref, fin_ref, final_norm):
    out = x + mod_ref[2:3, :] * y
    if final_norm:
        out = out * lax.rsqrt(jnp.mean(out * out, axis=-1, keepdims=True) + EPS) * fin_ref[...]
    return out


def _out_kernel(og_ref, x_ref, modc_ref, modl_ref, w_ref, b_ref, o_ref, *, n_ctx_tiles):
    y = jnp.dot(og_ref[...], w_ref[...], preferred_element_type=F32) + b_ref[...]
    is_ctx = pl.program_id(1) < n_ctx_tiles

    @pl.when(is_ctx)
    def _():
        o_ref[...] = _residual(x_ref[...], y, modc_ref, None, False)

    @pl.when(jnp.logical_not(is_ctx))
    def _():
        o_ref[...] = _residual(x_ref[...], y, modl_ref, None, False)


def _out_call(og, xa, modc, modl, w, layer, bias, n_ctx, skip_ctx):
    b, t, e = og.shape
    d = xa.shape[2]
    n_ctx_tiles = n_ctx // ROWS
    off = n_ctx_tiles if skip_ctx else 0
    n_tiles = t // ROWS - off
    return pl.pallas_call(
        functools.partial(_out_kernel, n_ctx_tiles=0 if skip_ctx else n_ctx_tiles),
        grid=(b, n_tiles),
        in_specs=[
            pl.BlockSpec((None, ROWS, e), lambda bi, i: (bi, i + off, 0)),
            pl.BlockSpec((None, ROWS, d), lambda bi, i: (bi, i + off, 0)),
            pl.BlockSpec((None, 3, d), lambda bi, i: (0, 0, 0)),
            pl.BlockSpec((None, 3, d), lambda bi, i: (bi, 0, 0)),
            pl.BlockSpec((None, e, d), lambda bi, i: (layer, 0, 0)),
            pl.BlockSpec((1, d), lambda bi, i: (0, 0)),
        ],
        out_specs=pl.BlockSpec((None, ROWS, d), lambda bi, i: (bi, i, 0)),
        out_shape=jax.ShapeDtypeStruct((b, n_tiles * ROWS, d), F32),
        compiler_params=_cparams("arbitrary", "arbitrary"),
        name="hgrn_out_proj",
    )(og, xa, modc, modl, w, bias)


def _ln_gate_project(c_scr, gate_slab, lng_ref, lnb_ref, w_ref, n_slabs, rows):
    e = n_slabs * CONV_LANES
    s1 = jnp.zeros((rows, 1), F32)
    for cc in range(n_slabs):
        s1 = s1 + jnp.sum(c_scr[cc], axis=-1, keepdims=True)
    mean = s1 / e
    s2 = jnp.zeros((rows, 1), F32)
    for cc in range(n_slabs):
        dv = c_scr[cc] - mean
        s2 = s2 + jnp.sum(dv * dv, axis=-1, keepdims=True)
    rstd = lax.rsqrt(s2 / e + EPS)
    y = None
    for cc in range(n_slabs):
        lanes = slice(cc * CONV_LANES, (cc + 1) * CONV_LANES)
        u = (c_scr[cc] - mean) * rstd * lng_ref[:, lanes] + lnb_ref[:, lanes]
        u = (_silu(u) * gate_slab(lanes)).astype(BF16)
        part = jnp.dot(u, w_ref[lanes, :], preferred_element_type=F32)
        y = part if y is None else y + part
    return y


def _convh_kernel(u_ref, gate_ref, x_ref, modc_ref, modl_ref, dw_ref, dwb_ref, lng_ref, lnb_ref,
                  w_ref, b_ref, sel_ref, o_ref, pad_scr, shift_scr, c_scr, *, n_ctx_tiles):
    n_slabs = pad_scr.shape[0]
    n_rows = ROWS // GRID_W
    halo = CONV_HALO
    is_ctx = pl.program_id(1) < n_ctx_tiles

    for cc in range(n_slabs):
        lanes = slice(cc * CONV_LANES, (cc + 1) * CONV_LANES)
        for r in range(n_rows):
            pad_scr[cc, r, halo:halo + GRID_W, :] = u_ref[r * GRID_W:(r + 1) * GRID_W, lanes]

    zeros = jnp.zeros((halo, CONV_LANES), BF16)

    @pl.when(is_ctx)
    def _():
        for cc in range(n_slabs):
            lanes = slice(cc * CONV_LANES, (cc + 1) * CONV_LANES)
            for r in range(n_rows):
                lo, hi = r * GRID_W, (r + 1) * GRID_W
                pad_scr[cc, r, 0:halo, :] = zeros if r == 0 else u_ref[lo - halo:lo, lanes]
                pad_scr[cc, r, halo + GRID_W:, :] = zeros if r == n_rows - 1 else u_ref[hi:hi + halo, lanes]

    @pl.when(jnp.logical_not(is_ctx))
    def _():
        for cc in range(n_slabs):
            for r in range(n_rows):
                pad_scr[cc, r, 0:halo, :] = zeros
                pad_scr[cc, r, halo + GRID_W:, :] = zeros

    def conv_slab(cc, carry):
        w = dw_ref[cc]
        for r in range(n_rows):
            acc = jnp.zeros((GRID_W, CONV_LANES), F32)
            span = CONV_SPAN
            shifted = jnp.dot(sel_ref[...], pad_scr[cc, r], preferred_element_type=F32)
            shift_scr[r] = shifted.reshape(SUBLANES, span, CONV_LANES)
            for res in range(SUBLANES):
                for off in range(0, span - GRID_W + 1, SUBLANES):
                    k = off + res - (halo - CONV_PAD)
                    if 0 <= k < CONV_W:
                        acc = acc + shift_scr[r, res, off:off + GRID_W, :] * w[k:k + 1, :]
            c_scr[cc, r * GRID_W:(r + 1) * GRID_W, :] = acc + dwb_ref[cc]
        return carry
    lax.fori_loop(0, n_slabs, conv_slab, 0)

    y = _ln_gate_project(c_scr, lambda lanes: gate_ref[:, lanes].astype(F32), lng_ref, lnb_ref,
                         w_ref, n_slabs, ROWS) + b_ref[...]

    @pl.when(is_ctx)
    def _():
        o_ref[...] = _residual(x_ref[...], y, modc_ref, None, False)

    @pl.when(jnp.logical_not(is_ctx))
    def _():
        o_ref[...] = _residual(x_ref[...], y, modl_ref, None, False)


def _convv_kernel(u_ref, gate_ref, x_ref, modl_ref, dw_ref, dwb_ref, lng_ref, lnb_ref,
                  w_ref, b_ref, fin_ref, o_ref, us_scr, c_scr):
    n_slabs = c_scr.shape[0]
    n_grid_rows = u_ref.shape[0]
    tokens = n_grid_rows * VCOLS

    for cc in range(n_slabs):
        us_scr[cc] = u_ref[:, :, cc * CONV_LANES:(cc + 1) * CONV_LANES].astype(F32)

    def conv_slab(cc, carry):
        w = dw_ref[cc]
        for r in range(n_grid_rows):
            acc = jnp.zeros((VCOLS, CONV_LANES), F32)
            for k in range(CONV_W):
                src = r + k - CONV_PAD
                if 0 <= src < n_grid_rows:
                    acc = acc + us_scr[cc, src] * w[k:k + 1, :]
            c_scr[cc, r * VCOLS:(r + 1) * VCOLS, :] = acc + dwb_ref[cc]
        return carry
    lax.fori_loop(0, n_slabs, conv_slab, 0)

    def gate_slab(lanes):
        return gate_ref[:, :, lanes].astype(F32).reshape(tokens, CONV_LANES)

    y = _ln_gate_project(c_scr, gate_slab, lng_ref, lnb_ref, w_ref, n_slabs, tokens) + b_ref[...]
    d = x_ref.shape[-1]
    out = _residual(x_ref[...].reshape(tokens, d), y, modl_ref, fin_ref, True)
    o_ref[...] = out.reshape(n_grid_rows, VCOLS, d)


def _slab_params(dw, dw_b):
    e = dw.shape[1]
    n_slabs = e // CONV_LANES
    dw_s = dw.reshape(CONV_W, n_slabs, CONV_LANES).transpose(1, 0, 2)
    return dw_s, dw_b.reshape(n_slabs, 1, CONV_LANES), n_slabs


def _convh_call(u, gate, xa, modc, modl, dw, dw_b, ln_g, ln_b, w, layer, bias, n_ctx):
    b, t, e = u.shape
    d = xa.shape[2]
    assert n_ctx in (0, ROWS) and ROWS % GRID_W == 0
    dw_s, dwb_s, n_slabs = _slab_params(dw, dw_b)
    tile = lambda width: pl.BlockSpec((None, ROWS, width), lambda bi, i: (bi, i, 0))
    full2 = lambda a: pl.BlockSpec(a.shape, lambda bi, i: (0, 0))
    full3 = lambda a: pl.BlockSpec(a.shape, lambda bi, i: (0, 0, 0))
    ln_g, ln_b, bias = ln_g.reshape(1, e), ln_b.reshape(1, e), bias.reshape(1, d)
    padded = GRID_W + 2 * CONV_HALO
    picked = (jnp.arange(SUBLANES)[:, None] + jnp.arange(CONV_SPAN)[None, :]).reshape(-1, 1)
    sel = (picked == jnp.arange(padded)[None, :]).astype(BF16)
    return pl.pallas_call(
        functools.partial(_convh_kernel, n_ctx_tiles=n_ctx // ROWS),
        grid=(b, t // ROWS),
        in_specs=[
            tile(e), tile(e), tile(d),
            pl.BlockSpec((None, 3, d), lambda bi, i: (0, 0, 0)),
            pl.BlockSpec((None, 3, d), lambda bi, i: (bi, 0, 0)),
            full3(dw_s), full3(dwb_s), full2(ln_g), full2(ln_b),
            pl.BlockSpec((None, e, d), lambda bi, i: (layer, 0, 0)),
            full2(bias), full2(sel),
        ],
        out_specs=tile(d),
        out_shape=jax.ShapeDtypeStruct((b, t, d), F32),
        scratch_shapes=[
            pltpu.VMEM((n_slabs, ROWS // GRID_W, padded, CONV_LANES), BF16),
            pltpu.VMEM((ROWS // GRID_W, SUBLANES, CONV_SPAN, CONV_LANES), F32),
            pltpu.VMEM((n_slabs, ROWS, CONV_LANES), F32),
        ],
        compiler_params=_cparams("arbitrary", "arbitrary"),
        name="conv_rows_out_proj",
    )(u, gate, xa, modc, modl, dw_s, dwb_s, ln_g, ln_b, w, bias, sel)


def _convv_call(u, gate, x, modl, dw, dw_b, ln_g, ln_b, w, layer, bias, final_g):
    b, l, e = u.shape
    d = x.shape[2]
    n_grid_rows = l // GRID_W
    dw_s, dwb_s, n_slabs = _slab_params(dw, dw_b)
    grid4 = lambda a: a.reshape(b, n_grid_rows, GRID_W, a.shape[-1])
    tile = lambda width: pl.BlockSpec((None, n_grid_rows, VCOLS, width), lambda bi, i: (bi, 0, i, 0))
    full2 = lambda a: pl.BlockSpec(a.shape, lambda bi, i: (0, 0))
    full3 = lambda a: pl.BlockSpec(a.shape, lambda bi, i: (0, 0, 0))
    ln_g, ln_b, bias, final_g = ln_g.reshape(1, e), ln_b.reshape(1, e), bias.reshape(1, d), final_g.reshape(1, d)
    out = pl.pallas_call(
        _convv_kernel,
        grid=(b, GRID_W // VCOLS),
        in_specs=[
            tile(e), tile(e), tile(d),
            pl.BlockSpec((None, 3, d), lambda bi, i: (bi, 0, 0)),
            full3(dw_s), full3(dwb_s), full2(ln_g), full2(ln_b),
            pl.BlockSpec((None, e, d), lambda bi, i: (layer, 0, 0)),
            full2(bias), full2(final_g),
        ],
        out_specs=tile(d),
        out_shape=jax.ShapeDtypeStruct((b, n_grid_rows, GRID_W, d), F32),
        scratch_shapes=[
            pltpu.VMEM((n_slabs, n_grid_rows, VCOLS, CONV_LANES), F32),
            pltpu.VMEM((n_slabs, n_grid_rows * VCOLS, CONV_LANES), F32),
        ],
        compiler_params=_cparams("arbitrary", "arbitrary"),
        name="conv_cols_out_proj_final_norm",
    )(grid4(u), grid4(gate), grid4(x), modl, dw_s, dwb_s, ln_g, ln_b, w, bias, final_g)
    return out.reshape(b, l, d)


def _hgrn_layer(xa, modc, modl, norm_g, w_in, lb, hnorm_g, w_out, j, n_ctx, last_recurrent):
    e = w_out.shape[1]
    d = xa.shape[2]
    q, v, gate, gf, kf, gb, kb = _proj_call(
        "hgrn_proj", xa, modc, modl, norm_g, [(w_in, j, i * e) for i in (0, 1, 4, 2, 3)],
        [lb[0, j].reshape(1, e), lb[1, j].reshape(1, e)],
        lambda accs, aux: _epi_qvgate(accs[:3], aux) + _epi_forget(accs[3:], aux),
        [BF16] * 3 + [F32, BF16, F32, BF16], e, n_ctx, tn=256)
    og = _gla_call(q, v, gate, gf, kf, gb, kb, hnorm_g[j], n_ctx)
    return _out_call(og, xa, modc, modl, w_out, j, jnp.zeros((1, d), F32), n_ctx, skip_ctx=last_recurrent)


def _conv_proj(xa, modc, modl, norm_g, w_in, b_in, j, n_ctx):
    e = w_in.shape[2] // 3
    bias = b_in[j].reshape(1, 3 * e)
    return _proj_call("conv_proj", xa, modc, modl, norm_g,
                      [(w_in, j, 0), (w_in, j, e), (w_in, j, 2 * e)],
                      [bias[:, :e], bias[:, e:2 * e], bias[:, 2 * e:]],
                      _epi_conv, [BF16, BF16], e, n_ctx, tn=256)


def kernel(x, c, ctx, c_ctx, norm_g, ada_w, ada_b, hgrn_w_in, hgrn_lb_logits, hgrn_norm_g, hgrn_w_out,
           conv_w_in, conv_b_in, conv_dw, conv_dw_b, conv_ln_g, conv_ln_b, conv_w_out, conv_b_out,
           final_norm_g):
    b, l, d = x.shape
    n_ctx = ctx.shape[1]
    depth = norm_g.shape[0]
    assert depth == 4, "layer schedule below is written for the 4-layer hybrid"

    bp = -(-(b + 1) // 8) * 8
    cs = jnp.concatenate([c, c_ctx[None, :], jnp.zeros((bp - b - 1, d), F32)], axis=0)
    mod = _ada_call(cs, ada_w, ada_b).reshape(depth, bp, 3, d)
    lb = _lb_call(hgrn_lb_logits)
    xa = jnp.concatenate([ctx, x], axis=1)
    hgrn_w_in, hgrn_w_out = hgrn_w_in.astype(BF16), hgrn_w_out.astype(BF16)
    conv_w_in, conv_w_out = conv_w_in.astype(BF16), conv_w_out.astype(BF16)

    def mods(i):
        return mod[i, b:b + 1], mod[i, :b]

    modc, modl = mods(0)
    xa = _hgrn_layer(xa, modc, modl, norm_g[0], hgrn_w_in, lb, hgrn_norm_g, hgrn_w_out, 0, n_ctx, False)
    modc, modl = mods(1)
    u, gate = _conv_proj(xa, modc, modl, norm_g[1], conv_w_in, conv_b_in, 0, n_ctx)
    xa = _convh_call(u, gate, xa, modc, modl, conv_dw[0], conv_dw_b[0], conv_ln_g[0], conv_ln_b[0],
                     conv_w_out, 0, conv_b_out[0], n_ctx)
    modc, modl = mods(2)
    xl = _hgrn_layer(xa, modc, modl, norm_g[2], hgrn_w_in, lb, hgrn_norm_g, hgrn_w_out, 1, n_ctx, True)
    modc, modl = mods(3)
    u, gate = _conv_proj(xl, modc, modl, norm_g[3], conv_w_in, conv_b_in, 1, 0)
    return _convv_call(u, gate, xl, modl, conv_dw[1], conv_dw_b[1], conv_ln_g[1], conv_ln_b[1],
                       conv_w_out, 1, conv_b_out[1], final_norm_g)
```

```python
import functools

import jax
import jax.numpy as jnp
from jax import lax
from jax.experimental import pallas as pl
from jax.experimental.pallas import tpu as pltpu

F32 = jnp.float32
BF16 = jnp.bfloat16

EPS = 1e-6
SUBLANES = 8
GRID_W = 64
HEAD = 128
CONV_W = 31
CONV_PAD = CONV_W // 2
CONV_HALO = 16
CONV_SPAN = GRID_W + SUBLANES * ((CONV_W + CONV_HALO - CONV_PAD) // SUBLANES - 1)
ROWS = 256
GLA_CHUNK = 64
GLA_SUB = 16
GLA_GROUP = 6
CONV_LANES = 256
VCOLS = 16
VMEM_LIMIT_BYTES = 56 * 1024 * 1024


def _cparams(*sem):
    return pltpu.CompilerParams(dimension_semantics=sem, vmem_limit_bytes=VMEM_LIMIT_BYTES)


def _sigmoid(x):
    return 1.0 / (1.0 + jnp.exp(-x))


def _silu(x):
    return x * _sigmoid(x)


def _ada_kernel(c_ref, w_ref, b_ref, o_ref):
    a = _silu(c_ref[...]).astype(BF16)
    o_ref[...] = jnp.dot(a, w_ref[...].astype(BF16), preferred_element_type=F32) + b_ref[...]


def _ada_call(cs, ada_w, ada_b):
    depth, d, n = ada_w.shape
    bp = cs.shape[0]
    tn = min(n, 1024)
    assert n % tn == 0
    return pl.pallas_call(
        _ada_kernel,
        grid=(depth, n // tn),
        in_specs=[
            pl.BlockSpec((bp, d), lambda l, j: (0, 0)),
            pl.BlockSpec((None, d, tn), lambda l, j: (l, 0, j)),
            pl.BlockSpec((None, 1, tn), lambda l, j: (l, 0, j)),
        ],
        out_specs=pl.BlockSpec((None, bp, tn), lambda l, j: (l, 0, j)),
        out_shape=jax.ShapeDtypeStruct((depth, bp, n), F32),
        compiler_params=_cparams("arbitrary", "arbitrary"),
        name="ada",
    )(cs, ada_w, ada_b.reshape(depth, 1, n))


def _lb_kernel(logit_ref, o_ref):
    n_a = logit_ref.shape[1]
    z = [logit_ref[:, i, :] for i in range(n_a)]
    m = functools.reduce(jnp.maximum, z)
    e = [jnp.exp(v - m) for v in z]
    tot = functools.reduce(lambda a, b: a + b, e)
    acc = jnp.zeros_like(tot)
    for i in range(n_a):
        if i > 0:
            acc = acc + e[i] / tot
        o_ref[:, i, :] = acc


def _lb_call(logits):
    return pl.pallas_call(
        _lb_kernel,
        out_shape=jax.ShapeDtypeStruct(logits.shape, F32),
        name="hgrn_lower_bounds",
    )(logits)


def _proj_kernel(*refs, n_w, n_aux, n_ctx_chunks, epilogue):
    x_ref, modc_ref, modl_ref, ng_ref = refs[:4]
    w_refs = refs[4:4 + n_w]
    aux_refs = refs[4 + n_w:4 + n_w + n_aux]
    out_refs = refs[4 + n_w + n_aux:-1]
    h_scr = refs[-1]
    n_chunks = x_ref.shape[0] // ROWS

    @pl.when(pl.program_id(1) == 0)
    def _():
        def norm_rows(mod_ref):
            gain = ng_ref[...] * (1.0 + mod_ref[1:2, :])

            def body(r, carry):
                rows = pl.ds(pl.multiple_of(r * ROWS, ROWS), ROWS)
                xv = x_ref[rows, :]
                y = xv * lax.rsqrt(jnp.mean(xv * xv, axis=-1, keepdims=True) + EPS)
                h_scr[rows, :] = (y * gain + mod_ref[0:1, :]).astype(BF16)
                return carry
            return body
        lax.fori_loop(0, n_ctx_chunks, norm_rows(modc_ref), 0)
        lax.fori_loop(n_ctx_chunks, n_chunks, norm_rows(modl_ref), 0)

    aux = [a[...] for a in aux_refs]

    def body(r, carry):
        rows = pl.ds(pl.multiple_of(r * ROWS, ROWS), ROWS)
        h = h_scr[rows, :]
        accs = [jnp.dot(h, w_ref[...], preferred_element_type=F32) for w_ref in w_refs]
        for o_ref, o in zip(out_refs, epilogue(accs, aux)):
            o_ref[rows, :] = o.astype(o_ref.dtype)
        return carry
    lax.fori_loop(0, n_chunks, body, 0, unroll=True)


def _proj_call(name, xa, modc, modl, norm_g, weights, auxs, epilogue, out_dtypes, n_out, n_ctx, tn):
    b, t, d = xa.shape
    assert t % ROWS == 0 and n_ctx % ROWS == 0 and n_out % tn == 0
    in_specs = [
        pl.BlockSpec((None, t, d), lambda bi, j: (bi, 0, 0)),
        pl.BlockSpec((None, 3, d), lambda bi, j: (0, 0, 0)),
        pl.BlockSpec((None, 3, d), lambda bi, j: (bi, 0, 0)),
        pl.BlockSpec((1, d), lambda bi, j: (0, 0)),
    ]
    args = [xa, modc, modl, norm_g.reshape(1, d)]
    for w, layer, col0 in weights:
        assert col0 % tn == 0
        in_specs.append(pl.BlockSpec((None, d, tn), functools.partial(
            lambda bi, j, layer, blk0: (layer, 0, blk0 + j), layer=layer, blk0=col0 // tn)))
        args.append(w)
    for a in auxs:
        in_specs.append(pl.BlockSpec((1, tn), lambda bi, j: (0, j)))
        args.append(a)
    kern = functools.partial(_proj_kernel, n_w=len(weights), n_aux=len(auxs),
                             n_ctx_chunks=n_ctx // ROWS, epilogue=epilogue)
    return pl.pallas_call(
        kern,
        grid=(b, n_out // tn),
        in_specs=in_specs,
        out_specs=[pl.BlockSpec((None, t, tn), lambda bi, j: (bi, 0, j)) for _ in out_dtypes],
        out_shape=[jax.ShapeDtypeStruct((b, t, n_out), dt) for dt in out_dtypes],
        scratch_shapes=[pltpu.VMEM((t, d), BF16)],
        compiler_params=_cparams("arbitrary", "arbitrary"),
        name=name,
    )(*args)


def _epi_qvgate(accs, aux):
    q, v, gate = accs
    return [_silu(q), v, _silu(gate)]


def _epi_forget(accs, aux):
    outs = []
    for z, lb in zip(accs, aux):
        f = lb + (1.0 - lb) * _sigmoid(z)
        outs += [jnp.log2(f), 1.0 - f]
    return outs


def _epi_conv(accs, aux):
    a, gl, gate = accs
    ba, bg, bgate = aux
    return (a + ba) * _sigmoid(gl + bg), _silu(gate + bgate)


def _cumsum_rows(g, reverse):
    n, width = g.shape
    row = lax.broadcasted_iota(jnp.int32, (SUBLANES, width), 0)
    groups = [g[j:j + SUBLANES, :] for j in range(0, n, SUBLANES)]
    sh = 1
    while sh < SUBLANES:
        if reverse:
            groups = [x + jnp.where(row < SUBLANES - sh, pltpu.roll(x, SUBLANES - sh, 0), 0.0) for x in groups]
        else:
            groups = [x + jnp.where(row >= sh, pltpu.roll(x, sh, 0), 0.0) for x in groups]
        sh *= 2
    out, carry = [], None
    for x in (reversed(groups) if reverse else groups):
        if carry is not None:
            x = x + carry
        carry = x[0:1, :] if reverse else x[SUBLANES - 1:SUBLANES, :]
        out.append(x)
    return jnp.concatenate(out[::-1] if reverse else out, axis=0)


_NT = (((1,), (1,)), ((), ()))


def _gla_local(q, kf, kb, g2f, g2b, masks):
    c, width = q.shape
    b_f = _cumsum_rows(g2f, reverse=False)
    b_b = _cumsum_rows(g2b, reverse=True)
    end_f, end_b = b_f[c - 1:c, :], b_b[0:1, :]
    both = lambda f, r: jnp.concatenate([f, r], axis=1)
    qi = both((q * jnp.exp2(b_f)).astype(BF16), (q * jnp.exp2(b_b)).astype(BF16))
    ke = both((kf * jnp.exp2(end_f - b_f)).astype(BF16), (kb * jnp.exp2(end_b - b_b)).astype(BF16))
    d = both(jnp.exp2(end_f), jnp.exp2(end_b))

    def scaled(x, b, m, is_query, ref_row):
        out = []
        for i in range(c // m):
            r = ref_row(i)
            if r is None:
                out.append(jnp.zeros((m, width), BF16))
            else:
                blk = slice(i * m, (i + 1) * m)
                e = b[blk, :] - b[r:r + 1, :] if is_query else b[r:r + 1, :] - b[blk, :]
                out.append((x[blk, :] * jnp.exp2(e)).astype(BF16))
        return jnp.concatenate(out, axis=0)

    def scores(qs, ks):
        return lax.dot_general(qs, ks, _NT, preferred_element_type=F32)

    m = GLA_SUB
    mid = lambda i: i * m + m // 2
    p = (jnp.where(masks[0], scores(scaled(q, b_f, m, True, mid), scaled(kf, b_f, m, False, mid)), 0.0)
         + jnp.where(masks[1], scores(scaled(q, b_b, m, True, mid), scaled(kb, b_b, m, False, mid)), 0.0))
    level = 2
    while m < c:
        odd, even = (lambda i: i % 2 == 1), (lambda i: i % 2 == 0)
        pick = lambda use, row: (lambda i: row(i) if use(i) else None)
        qs = both(scaled(q, b_f, m, True, pick(odd, lambda i: i * m - 1)),
                  scaled(q, b_b, m, True, pick(even, lambda i: (i + 1) * m)))
        ks = both(scaled(kf, b_f, m, False, pick(even, lambda i: (i + 1) * m - 1)),
                  scaled(kb, b_b, m, False, pick(odd, lambda i: i * m)))
        s = scores(qs, ks)
        p = p + s if 2 * m == c else jnp.where(masks[level], s, p)
        m *= 2
        level += 1
    return p, qi, ke, d


def _gla_masks(c):
    t_idx = lax.broadcasted_iota(jnp.int32, (c, c), 0)
    s_idx = lax.broadcasted_iota(jnp.int32, (c, c), 1)
    m = GLA_SUB
    shift = m.bit_length() - 1
    same = (t_idx >> shift) == (s_idx >> shift)
    masks = [jnp.logical_and(same, s_idx <= t_idx), jnp.logical_and(same, s_idx >= t_idx)]
    while 2 * m < c:
        masks.append(((t_idx >> shift) ^ (s_idx >> shift)) == 1)
        m *= 2
        shift += 1
    return masks


def _gla_kernel(q_ref, v_ref, gate_ref, gf_ref, kf_ref, gb_ref, kb_ref, ng_ref, o_ref,
                oi_scr, qi_scr, ke_scr, p_scr, u_scr, d_scr, st_scr, *, n_ctx):
    t = q_ref.shape[0]
    n_chunks = t // GLA_CHUNK
    n_ctx_chunks = n_ctx // GLA_CHUNK
    n_groups = n_chunks // GLA_GROUP
    fwd, bwd = slice(0, HEAD), slice(HEAD, 2 * HEAD)

    def chunk_rows(n):
        start = n * GLA_CHUNK
        return pl.ds(start if isinstance(n, int) else pl.multiple_of(start, GLA_CHUNK), GLA_CHUNK)

    def group_chunks(group):
        return (group * GLA_GROUP + j for j in range(GLA_GROUP))

    masks = _gla_masks(GLA_CHUNK)

    def local(group):
        for n in group_chunks(group):
            rows = chunk_rows(n)
            p, qi, ke, d = _gla_local(q_ref[rows, :].astype(F32), kf_ref[rows, :].astype(F32),
                                      kb_ref[rows, :].astype(F32), gf_ref[rows, :], gb_ref[rows, :], masks)
            p_scr[n] = p.astype(BF16)
            qi_scr[rows, :] = qi
            ke_scr[rows, :] = ke
            d_scr[n] = d

    def local_products(group):
        for n in group_chunks(group):
            rows = chunk_rows(n)
            v = v_ref[rows, :]
            oi_scr[rows, :] = jnp.dot(p_scr[n], v, preferred_element_type=F32)
            u_scr[n] = jnp.dot(v.astype(F32).T.astype(BF16), ke_scr[rows, :], preferred_element_type=F32)

    def skewed(group, carry):
        local_products(group - 1)
        local(group)
        return carry

    local(0)
    lax.fori_loop(1, n_groups, skewed, 0)
    local_products(n_groups - 1)

    def scan(lanes, first_chunk, step, count, st):
        def body(i, st):
            n = first_chunk + step * i
            st_scr[n, :, lanes] = st.astype(BF16)
            return st * d_scr[n, :, lanes] + u_scr[n, :, lanes]
        return lax.fori_loop(0, count, body, st, unroll=2)

    st0 = jnp.zeros((HEAD, HEAD), F32)
    scan(fwd, 0, 1, n_chunks, st0)
    st_b = scan(bwd, n_ctx_chunks - 1, -1, n_ctx_chunks, st0)
    scan(bwd, n_chunks - 1, -1, n_chunks - n_ctx_chunks, st_b)

    def readout(n, carry):
        rows = chunk_rows(n)
        o = oi_scr[rows, :] + lax.dot_general(qi_scr[rows, :], st_scr[n], _NT, preferred_element_type=F32)
        o = o * lax.rsqrt(jnp.mean(o * o, axis=-1, keepdims=True) + EPS) * ng_ref[...]
        o_ref[rows, :] = (o * gate_ref[rows, :].astype(F32)).astype(o_ref.dtype)
        return carry
    lax.fori_loop(0, n_chunks, readout, 0, unroll=18)


def _gla_call(q, v, gate, gf, kf, gb, kb, norm_g, n_ctx):
    b, t, e = q.shape
    assert t % GLA_CHUNK == 0 and n_ctx % GLA_CHUNK == 0 and (t // GLA_CHUNK) % GLA_GROUP == 0
    n_chunks = t // GLA_CHUNK
    spec = pl.BlockSpec((None, t, HEAD), lambda bi, h: (bi, 0, h))
    return pl.pallas_call(
        functools.partial(_gla_kernel, n_ctx=n_ctx),
        grid=(b, e // HEAD),
        in_specs=[spec] * 7 + [pl.BlockSpec((1, HEAD), lambda bi, h: (0, 0))],
        out_specs=spec,
        out_shape=jax.ShapeDtypeStruct((b, t, e), BF16),
        scratch_shapes=[
            pltpu.VMEM((t, HEAD), F32),
            pltpu.VMEM((t, 2 * HEAD), BF16),
            pltpu.VMEM((t, 2 * HEAD), BF16),
            pltpu.VMEM((n_chunks, GLA_CHUNK, GLA_CHUNK), BF16),
            pltpu.VMEM((n_chunks, HEAD, 2 * HEAD), F32),
            pltpu.VMEM((n_chunks, 1, 2 * HEAD), F32),
            pltpu.VMEM((n_chunks, HEAD, 2 * HEAD), BF16),
        ],
        compiler_params=_cparams("arbitrary", "arbitrary"),
        name="hgrn_recurrence",
    )(q, v, gate, gf, kf, gb, kb, norm_g.reshape(1, HEAD))


def _residual(x, y, mod_ref, fin_ref, final_norm):
    out = x + mod_ref[2:3, :] * y
    if final_norm:
        out = out * lax.rsqrt(jnp.mean(out * out, axis=-1, keepdims=True) + EPS) * fin_ref[...]
    return out


def _out_kernel(og_ref, x_ref, modc_ref, modl_ref, w_ref, b_ref, o_ref, *, n_ctx_tiles):
    y = jnp.dot(og_ref[...], w_ref[...], preferred_element_type=F32) + b_ref[...]
    is_ctx = pl.program_id(1) < n_ctx_tiles

    @pl.when(is_ctx)
    def _():
        o_ref[...] = _residual(x_ref[...], y, modc_ref, None, False)

    @pl.when(jnp.logical_not(is_ctx))
    def _():
        o_ref[...] = _residual(x_ref[...], y, modl_ref, None, False)


def _out_call(og, xa, modc, modl, w, layer, bias, n_ctx, skip_ctx):
    b, t, e = og.shape
    d = xa.shape[2]
    n_ctx_tiles = n_ctx // ROWS
    off = n_ctx_tiles if skip_ctx else 0
    n_tiles = t // ROWS - off
    return pl.pallas_call(
        functools.partial(_out_kernel, n_ctx_tiles=0 if skip_ctx else n_ctx_tiles),
        grid=(b, n_tiles),
        in_specs=[
            pl.BlockSpec((None, ROWS, e), lambda bi, i: (bi, i + off, 0)),
            pl.BlockSpec((None, ROWS, d), lambda bi, i: (bi, i + off, 0)),
            pl.BlockSpec((None, 3, d), lambda bi, i: (0, 0, 0)),
            pl.BlockSpec((None, 3, d), lambda bi, i: (bi, 0, 0)),
            pl.BlockSpec((None, e, d), lambda bi, i: (layer, 0, 0)),
            pl.BlockSpec((1, d), lambda bi, i: (0, 0)),
        ],
        out_specs=pl.BlockSpec((None, ROWS, d), lambda bi, i: (bi, i, 0)),
        out_shape=jax.ShapeDtypeStruct((b, n_tiles * ROWS, d), F32),
        compiler_params=_cparams("arbitrary", "arbitrary"),
        name="hgrn_out_proj",
    )(og, xa, modc, modl, w, bias)


def _ln_gate_project(c_scr, gate_slab, lng_ref, lnb_ref, w_ref, n_slabs, rows):
    e = n_slabs * CONV_LANES
    s1 = jnp.zeros((rows, 1), F32)
    for cc in range(n_slabs):
        s1 = s1 + jnp.sum(c_scr[cc], axis=-1, keepdims=True)
    mean = s1 / e
    s2 = jnp.zeros((rows, 1), F32)
    for cc in range(n_slabs):
        dv = c_scr[cc] - mean
        s2 = s2 + jnp.sum(dv * dv, axis=-1, keepdims=True)
    rstd = lax.rsqrt(s2 / e + EPS)
    y = None
    for cc in range(n_slabs):
        lanes = slice(cc * CONV_LANES, (cc + 1) * CONV_LANES)
        u = (c_scr[cc] - mean) * rstd * lng_ref[:, lanes] + lnb_ref[:, lanes]
        u = (_silu(u) * gate_slab(lanes)).astype(BF16)
        part = jnp.dot(u, w_ref[lanes, :], preferred_element_type=F32)
        y = part if y is None else y + part
    return y


def _convh_kernel(u_ref, gate_ref, x_ref, modc_ref, modl_ref, dw_ref, dwb_ref, lng_ref, lnb_ref,
                  w_ref, b_ref, sel_ref, o_ref, pad_scr, shift_scr, c_scr, *, n_ctx_tiles):
    n_slabs = pad_scr.shape[0]
    n_rows = ROWS // GRID_W
    halo = CONV_HALO
    is_ctx = pl.program_id(1) < n_ctx_tiles

    for cc in range(n_slabs):
        lanes = slice(cc * CONV_LANES, (cc + 1) * CONV_LANES)
        for r in range(n_rows):
            pad_scr[cc, r, halo:halo + GRID_W, :] = u_ref[r * GRID_W:(r + 1) * GRID_W, lanes]

    zeros = jnp.zeros((halo, CONV_LANES), BF16)

    @pl.when(is_ctx)
    def _():
        for cc in range(n_slabs):
            lanes = slice(cc * CONV_LANES, (cc + 1) * CONV_LANES)
            for r in range(n_rows):
                lo, hi = r * GRID_W, (r + 1) * GRID_W
                pad_scr[cc, r, 0:halo, :] = zeros if r == 0 else u_ref[lo - halo:lo, lanes]
                pad_scr[cc, r, halo + GRID_W:, :] = zeros if r == n_rows - 1 else u_ref[hi:hi + halo, lanes]

    @pl.when(jnp.logical_not(is_ctx))
    def _():
        for cc in range(n_slabs):
            for r in range(n_rows):
                pad_scr[cc, r, 0:halo, :] = zeros
                pad_scr[cc, r, halo + GRID_W:, :] = zeros

    def conv_slab(cc, carry):
        w = dw_ref[cc]
        for r in range(n_rows):
            acc = jnp.zeros((GRID_W, CONV_LANES), F32)
            span = CONV_SPAN
            shifted = jnp.dot(sel_ref[...], pad_scr[cc, r], preferred_element_type=F32)
            shift_scr[r] = shifted.reshape(SUBLANES, span, CONV_LANES)
            for res in range(SUBLANES):
                for off in range(0, span - GRID_W + 1, SUBLANES):
                    k = off + res - (halo - CONV_PAD)
                    if 0 <= k < CONV_W:
                        acc = acc + shift_scr[r, res, off:off + GRID_W, :] * w[k:k + 1, :]
            c_scr[cc, r * GRID_W:(r + 1) * GRID_W, :] = acc + dwb_ref[cc]
        return carry
    lax.fori_loop(0, n_slabs, conv_slab, 0)

    y = _ln_gate_project(c_scr, lambda lanes: gate_ref[:, lanes].astype(F32), lng_ref, lnb_ref,
                         w_ref, n_slabs, ROWS) + b_ref[...]

    @pl.when(is_ctx)
    def _():
        o_ref[...] = _residual(x_ref[...], y, modc_ref, None, False)

    @pl.when(jnp.logical_not(is_ctx))
    def _():
        o_ref[...] = _residual(x_ref[...], y, modl_ref, None, False)


def _convv_kernel(u_ref, gate_ref, x_ref, modl_ref, dw_ref, dwb_ref, lng_ref, lnb_ref,
                  w_ref, b_ref, fin_ref, o_ref, us_scr, c_scr):
    n_slabs = c_scr.shape[0]
    n_grid_rows = u_ref.shape[0]
    tokens = n_grid_rows * VCOLS

    for cc in range(n_slabs):
        us_scr[cc] = u_ref[:, :, cc * CONV_LANES:(cc + 1) * CONV_LANES].astype(F32)

    def conv_slab(cc, carry):
        w = dw_ref[cc]
        for r in range(n_grid_rows):
            acc = jnp.zeros((VCOLS, CONV_LANES), F32)
            for k in range(CONV_W):
                src = r + k - CONV_PAD
                if 0 <= src < n_grid_rows:
                    acc = acc + us_scr[cc, src] * w[k:k + 1, :]
            c_scr[cc, r * VCOLS:(r + 1) * VCOLS, :] = acc + dwb_ref[cc]
        return carry
    lax.fori_loop(0, n_slabs, conv_slab, 0)

    def gate_slab(lanes):
        return gate_ref[:, :, lanes].astype(F32).reshape(tokens, CONV_LANES)

    y = _ln_gate_project(c_scr, gate_slab, lng_ref, lnb_ref, w_ref, n_slabs, tokens) + b_ref[...]
    d = x_ref.shape[-1]
    out = _residual(x_ref[...].reshape(tokens, d), y, modl_ref, fin_ref, True)
    o_ref[...] = out.reshape(n_grid_rows, VCOLS, d)


def _slab_params(dw, dw_b):
    e = dw.shape[1]
    n_slabs = e // CONV_LANES
    dw_s = dw.reshape(CONV_W, n_slabs, CONV_LANES).transpose(1, 0, 2)
    return dw_s, dw_b.reshape(n_slabs, 1, CONV_LANES), n_slabs


def _convh_call(u, gate, xa, modc, modl, dw, dw_b, ln_g, ln_b, w, layer, bias, n_ctx):
    b, t, e = u.shape
    d = xa.shape[2]
    assert n_ctx in (0, ROWS) and ROWS % GRID_W == 0
    dw_s, dwb_s, n_slabs = _slab_params(dw, dw_b)
    tile = lambda width: pl.BlockSpec((None, ROWS, width), lambda bi, i: (bi, i, 0))
    full2 = lambda a: pl.BlockSpec(a.shape, lambda bi, i: (0, 0))
    full3 = lambda a: pl.BlockSpec(a.shape, lambda bi, i: (0, 0, 0))
    ln_g, ln_b, bias = ln_g.reshape(1, e), ln_b.reshape(1, e), bias.reshape(1, d)
    padded = GRID_W + 2 * CONV_HALO
    picked = (jnp.arange(SUBLANES)[:, None] + jnp.arange(CONV_SPAN)[None, :]).reshape(-1, 1)
    sel = (picked == jnp.arange(padded)[None, :]).astype(BF16)
    return pl.pallas_call(
        functools.partial(_convh_kernel, n_ctx_tiles=n_ctx // ROWS),
        grid=(b, t // ROWS),
        in_specs=[
            tile(e), tile(e), tile(d),
            pl.BlockSpec((None, 3, d), lambda bi, i: (0, 0, 0)),
            pl.BlockSpec((None, 3, d), lambda bi, i: (bi, 0, 0)),
            full3(dw_s), full3(dwb_s), full2(ln_g), full2(ln_b),
            pl.BlockSpec((None, e, d), lambda bi, i: (layer, 0, 0)),
            full2(bias), full2(sel),
        ],
        out_specs=tile(d),
        out_shape=jax.ShapeDtypeStruct((b, t, d), F32),
        scratch_shapes=[
            pltpu.VMEM((n_slabs, ROWS // GRID_W, padded, CONV_LANES), BF16),
            pltpu.VMEM((ROWS // GRID_W, SUBLANES, CONV_SPAN, CONV_LANES), F32),
            pltpu.VMEM((n_slabs, ROWS, CONV_LANES), F32),
        ],
        compiler_params=_cparams("arbitrary", "arbitrary"),
        name="conv_rows_out_proj",
    )(u, gate, xa, modc, modl, dw_s, dwb_s, ln_g, ln_b, w, bias, sel)


def _convv_call(u, gate, x, modl, dw, dw_b, ln_g, ln_b, w, layer, bias, final_g):
    b, l, e = u.shape
    d = x.shape[2]
    n_grid_rows = l // GRID_W
    dw_s, dwb_s, n_slabs = _slab_params(dw, dw_b)
    grid4 = lambda a: a.reshape(b, n_grid_rows, GRID_W, a.shape[-1])
    tile = lambda width: pl.BlockSpec((None, n_grid_rows, VCOLS, width), lambda bi, i: (bi, 0, i, 0))
    full2 = lambda a: pl.BlockSpec(a.shape, lambda bi, i: (0, 0))
    full3 = lambda a: pl.BlockSpec(a.shape, lambda bi, i: (0, 0, 0))
    ln_g, ln_b, bias, final_g = ln_g.reshape(1, e), ln_b.reshape(1, e), bias.reshape(1, d), final_g.reshape(1, d)
    out = pl.pallas_call(
        _convv_kernel,
        grid=(b, GRID_W // VCOLS),
        in_specs=[
            tile(e), tile(e), tile(d),
            pl.BlockSpec((None, 3, d), lambda bi, i: (bi, 0, 0)),
            full3(dw_s), full3(dwb_s), full2(ln_g), full2(ln_b),
            pl.BlockSpec((None, e, d), lambda bi, i: (layer, 0, 0)),
            full2(bias), full2(final_g),
        ],
        out_specs=tile(d),
        out_shape=jax.ShapeDtypeStruct((b, n_grid_rows, GRID_W, d), F32),
        scratch_shapes=[
            pltpu.VMEM((n_slabs, n_grid_rows, VCOLS, CONV_LANES), F32),
            pltpu.VMEM((n_slabs, n_grid_rows * VCOLS, CONV_LANES), F32),
        ],
        compiler_params=_cparams("arbitrary", "arbitrary"),
        name="conv_cols_out_proj_final_norm",
    )(grid4(u), grid4(gate), grid4(x), modl, dw_s, dwb_s, ln_g, ln_b, w, bias, final_g)
    return out.reshape(b, l, d)


def _hgrn_layer(xa, modc, modl, norm_g, w_in, lb, hnorm_g, w_out, j, n_ctx, last_recurrent):
    e = w_out.shape[1]
    d = xa.shape[2]
    q, v, gate, gf, kf, gb, kb = _proj_call(
        "hgrn_proj", xa, modc, modl, norm_g, [(w_in, j, i * e) for i in (0, 1, 4, 2, 3)],
        [lb[0, j].reshape(1, e), lb[1, j].reshape(1, e)],
        lambda accs, aux: _epi_qvgate(accs[:3], aux) + _epi_forget(accs[3:], aux),
        [BF16] * 3 + [F32, BF16, F32, BF16], e, n_ctx, tn=256)
    og = _gla_call(q, v, gate, gf, kf, gb, kb, hnorm_g[j], n_ctx)
    return _out_call(og, xa, modc, modl, w_out, j, jnp.zeros((1, d), F32), n_ctx, skip_ctx=last_recurrent)


def _conv_proj(xa, modc, modl, norm_g, w_in, b_in, j, n_ctx):
    e = w_in.shape[2] // 3
    bias = b_in[j].reshape(1, 3 * e)
    return _proj_call("conv_proj", xa, modc, modl, norm_g,
                      [(w_in, j, 0), (w_in, j, e), (w_in, j, 2 * e)],
                      [bias[:, :e], bias[:, e:2 * e], bias[:, 2 * e:]],
                      _epi_conv, [BF16, BF16], e, n_ctx, tn=256)


def kernel(x, c, ctx, c_ctx, norm_g, ada_w, ada_b, hgrn_w_in, hgrn_lb_logits, hgrn_norm_g, hgrn_w_out,
           conv_w_in, conv_b_in, conv_dw, conv_dw_b, conv_ln_g, conv_ln_b, conv_w_out, conv_b_out,
           final_norm_g):
    b, l, d = x.shape
    n_ctx = ctx.shape[1]
    depth = norm_g.shape[0]
    assert depth == 4, "layer schedule below is written for the 4-layer hybrid"

    bp = -(-(b + 1) // 8) * 8
    cs = jnp.concatenate([c, c_ctx[None, :], jnp.zeros((bp - b - 1, d), F32)], axis=0)
    mod = _ada_call(cs, ada_w, ada_b).reshape(depth, bp, 3, d)
    lb = _lb_call(hgrn_lb_logits)
    xa = jnp.concatenate([ctx, x], axis=1)
    hgrn_w_in, hgrn_w_out = hgrn_w_in.astype(BF16), hgrn_w_out.astype(BF16)
    conv_w_in, conv_w_out = conv_w_in.astype(BF16), conv_w_out.astype(BF16)

    def mods(i):
        return mod[i, b:b + 1], mod[i, :b]

    modc, modl = mods(0)
    xa = _hgrn_layer(xa, modc, modl, norm_g[0], hgrn_w_in, lb, hgrn_norm_g, hgrn_w_out, 0, n_ctx, False)
    modc, modl = mods(1)
    u, gate = _conv_proj(xa, modc, modl, norm_g[1], conv_w_in, conv_b_in, 0, n_ctx)
    xa = _convh_call(u, gate, xa, modc, modl, conv_dw[0], conv_dw_b[0], conv_ln_g[0], conv_ln_b[0],
                     conv_w_out, 0, conv_b_out[0], n_ctx)
    modc, modl = mods(2)
    xl = _hgrn_layer(xa, modc, modl, norm_g[2], hgrn_w_in, lb, hgrn_norm_g, hgrn_w_out, 1, n_ctx, True)
    modc, modl = mods(3)
    u, gate = _conv_proj(xl, modc, modl, norm_g[3], conv_w_in, conv_b_in, 1, 0)
    return _convv_call(u, gate, xl, modl, conv_dw[1], conv_dw_b[1], conv_ln_g[1], conv_ln_b[1],
                       conv_w_out, 1, conv_b_out[1], final_norm_g)
```

```python
import functools

import jax
import jax.numpy as jnp
from jax import lax
from jax.experimental import pallas as pl
from jax.experimental.pallas import tpu as pltpu

F32 = jnp.float32
BF16 = jnp.bfloat16

EPS = 1e-6
SUBLANES = 8
GRID_W = 64
HEAD = 128
CONV_W = 31
CONV_PAD = CONV_W // 2
CONV_HALO = 16
CONV_SPAN = GRID_W + SUBLANES * ((CONV_W + CONV_HALO - CONV_PAD) // SUBLANES - 1)
ROWS = 256
GLA_CHUNK = 64
GLA_SUB = 16
GLA_GROUP = 6
CONV_LANES = 256
VCOLS = 16
VMEM_LIMIT_BYTES = 56 * 1024 * 1024


def _cparams(*sem):
    return pltpu.CompilerParams(dimension_semantics=sem, vmem_limit_bytes=VMEM_LIMIT_BYTES)


def _sigmoid(x):
    return 1.0 / (1.0 + jnp.exp(-x))


def _silu(x):
    return x * _sigmoid(x)


def _ada_kernel(c_ref, w_ref, b_ref, o_ref):
    a = _silu(c_ref[...]).astype(BF16)
    o_ref[...] = jnp.dot(a, w_ref[...].astype(BF16), preferred_element_type=F32) + b_ref[...]


def _ada_call(cs, ada_w, ada_b):
    depth, d, n = ada_w.shape
    bp = cs.shape[0]
    tn = min(n, 1024)
    assert n % tn == 0
    return pl.pallas_call(
        _ada_kernel,
        grid=(depth, n // tn),
        in_specs=[
            pl.BlockSpec((bp, d), lambda l, j: (0, 0)),
            pl.BlockSpec((None, d, tn), lambda l, j: (l, 0, j)),
            pl.BlockSpec((None, 1, tn), lambda l, j: (l, 0, j)),
        ],
        out_specs=pl.BlockSpec((None, bp, tn), lambda l, j: (l, 0, j)),
        out_shape=jax.ShapeDtypeStruct((depth, bp, n), F32),
        compiler_params=_cparams("arbitrary", "arbitrary"),
        name="ada",
    )(cs, ada_w, ada_b.reshape(depth, 1, n))


def _lb_kernel(logit_ref, o_ref):
    n_a = logit_ref.shape[1]
    z = [logit_ref[:, i, :] for i in range(n_a)]
    m = functools.reduce(jnp.maximum, z)
    e = [jnp.exp(v - m) for v in z]
    tot = functools.reduce(lambda a, b: a + b, e)
    acc = jnp.zeros_like(tot)
    for i in range(n_a):
        if i > 0:
            acc = acc + e[i] / tot
        o_ref[:, i, :] = acc


def _lb_call(logits):
    return pl.pallas_call(
        _lb_kernel,
        out_shape=jax.ShapeDtypeStruct(logits.shape, F32),
        name="hgrn_lower_bounds",
    )(logits)


def _proj_kernel(*refs, n_w, n_aux, n_ctx_chunks, epilogue):
    x_ref, modc_ref, modl_ref, ng_ref = refs[:4]
    w_refs = refs[4:4 + n_w]
    aux_refs = refs[4 + n_w:4 + n_w + n_aux]
    out_refs = refs[4 + n_w + n_aux:-1]
    h_scr = refs[-1]
    n_chunks = x_ref.shape[0] // ROWS

    @pl.when(pl.program_id(1) == 0)
    def _():
        def norm_rows(mod_ref):
            gain = ng_ref[...] * (1.0 + mod_ref[1:2, :])

            def body(r, carry):
                rows = pl.ds(pl.multiple_of(r * ROWS, ROWS), ROWS)
                xv = x_ref[rows, :]
                y = xv * lax.rsqrt(jnp.mean(xv * xv, axis=-1, keepdims=True) + EPS)
                h_scr[rows, :] = (y * gain + mod_ref[0:1, :]).astype(BF16)
                return carry
            return body
        lax.fori_loop(0, n_ctx_chunks, norm_rows(modc_ref), 0)
        lax.fori_loop(n_ctx_chunks, n_chunks, norm_rows(modl_ref), 0)

    aux = [a[...] for a in aux_refs]

    def body(r, carry):
        rows = pl.ds(pl.multiple_of(r * ROWS, ROWS), ROWS)
        h = h_scr[rows, :]
        accs = [jnp.dot(h, w_ref[...], preferred_element_type=F32) for w_ref in w_refs]
        for o_ref, o in zip(out_refs, epilogue(accs, aux)):
            o_ref[rows, :] = o.astype(o_ref.dtype)
        return carry
    lax.fori_loop(0, n_chunks, body, 0, unroll=True)


def _proj_call(name, xa, modc, modl, norm_g, weights, auxs, epilogue, out_dtypes, n_out, n_ctx, tn):
    b, t, d = xa.shape
    assert t % ROWS == 0 and n_ctx % ROWS == 0 and n_out % tn == 0
    in_specs = [
        pl.BlockSpec((None, t, d), lambda bi, j: (bi, 0, 0)),
        pl.BlockSpec((None, 3, d), lambda bi, j: (0, 0, 0)),
        pl.BlockSpec((None, 3, d), lambda bi, j: (bi, 0, 0)),
        pl.BlockSpec((1, d), lambda bi, j: (0, 0)),
    ]
    args = [xa, modc, modl, norm_g.reshape(1, d)]
    for w, layer, col0 in weights:
        assert col0 % tn == 0
        in_specs.append(pl.BlockSpec((None, d, tn), functools.partial(
            lambda bi, j, layer, blk0: (layer, 0, blk0 + j), layer=layer, blk0=col0 // tn)))
        args.append(w)
    for a in auxs:
        in_specs.append(pl.BlockSpec((1, tn), lambda bi, j: (0, j)))
        args.append(a)
    kern = functools.partial(_proj_kernel, n_w=len(weights), n_aux=len(auxs),
                             n_ctx_chunks=n_ctx // ROWS, epilogue=epilogue)
    return pl.pallas_call(
        kern,
        grid=(b, n_out // tn),
        in_specs=in_specs,
        out_specs=[pl.BlockSpec((None, t, tn), lambda bi, j: (bi, 0, j)) for _ in out_dtypes],
        out_shape=[jax.ShapeDtypeStruct((b, t, n_out), dt) for dt in out_dtypes],
        scratch_shapes=[pltpu.VMEM((t, d), BF16)],
        compiler_params=_cparams("arbitrary", "arbitrary"),
        name=name,
    )(*args)


def _epi_qvgate(accs, aux):
    q, v, gate = accs
    return [_silu(q), v, _silu(gate)]


def _epi_forget(accs, aux):
    outs = []
    for z, lb in zip(accs, aux):
        f = lb + (1.0 - lb) * _sigmoid(z)
        outs += [jnp.log2(f), 1.0 - f]
    return outs


def _epi_conv(accs, aux):
    a, gl, gate = accs
    ba, bg, bgate = aux
    return (a + ba) * _sigmoid(gl + bg), _silu(gate + bgate)


def _cumsum_rows(g, reverse):
    n, width = g.shape
    row = lax.broadcasted_iota(jnp.int32, (SUBLANES, width), 0)
    groups = [g[j:j + SUBLANES, :] for j in range(0, n, SUBLANES)]
    sh = 1
    while sh < SUBLANES:
        if reverse:
            groups = [x + jnp.where(row < SUBLANES - sh, pltpu.roll(x, SUBLANES - sh, 0), 0.0) for x in groups]
        else:
            groups = [x + jnp.where(row >= sh, pltpu.roll(x, sh, 0), 0.0) for x in groups]
        sh *= 2
    out, carry = [], None
    for x in (reversed(groups) if reverse else groups):
        if carry is not None:
            x = x + carry
        carry = x[0:1, :] if reverse else x[SUBLANES - 1:SUBLANES, :]
        out.append(x)
    return jnp.concatenate(out[::-1] if reverse else out, axis=0)


_NT = (((1,), (1,)), ((), ()))


def _gla_local(q, kf, kb, g2f, g2b, masks):
    c, width = q.shape
    b_f = _cumsum_rows(g2f, reverse=False)
    b_b = _cumsum_rows(g2b, reverse=True)
    end_f, end_b = b_f[c - 1:c, :], b_b[0:1, :]
    both = lambda f, r: jnp.concatenate([f, r], axis=1)
    qi = both((q * jnp.exp2(b_f)).astype(BF16), (q * jnp.exp2(b_b)).astype(BF16))
    ke = both((kf * jnp.exp2(end_f - b_f)).astype(BF16), (kb * jnp.exp2(end_b - b_b)).astype(BF16))
    d = both(jnp.exp2(end_f), jnp.exp2(end_b))

    def scaled(x, b, m, is_query, ref_row):
        out = []
        for i in range(c // m):
            r = ref_row(i)
            if r is None:
                out.append(jnp.zeros((m, width), BF16))
            else:
                blk = slice(i * m, (i + 1) * m)
                e = b[blk, :] - b[r:r + 1, :] if is_query else b[r:r + 1, :] - b[blk, :]
                out.append((x[blk, :] * jnp.exp2(e)).astype(BF16))
        return jnp.concatenate(out, axis=0)

    def scores(qs, ks):
        return lax.dot_general(qs, ks, _NT, preferred_element_type=F32)

    m = GLA_SUB
    mid = lambda i: i * m + m // 2
    p = (jnp.where(masks[0], scores(scaled(q, b_f, m, True, mid), scaled(kf, b_f, m, False, mid)), 0.0)
         + jnp.where(masks[1], scores(scaled(q, b_b, m, True, mid), scaled(kb, b_b, m, False, mid)), 0.0))
    level = 2
    while m < c:
        odd, even = (lambda i: i % 2 == 1), (lambda i: i % 2 == 0)
        pick = lambda use, row: (lambda i: row(i) if use(i) else None)
        qs = both(scaled(q, b_f, m, True, pick(odd, lambda i: i * m - 1)),
                  scaled(q, b_b, m, True, pick(even, lambda i: (i + 1) * m)))
        ks = both(scaled(kf, b_f, m, False, pick(even, lambda i: (i + 1) * m - 1)),
                  scaled(kb, b_b, m, False, pick(odd, lambda i: i * m)))
        s = scores(qs, ks)
        p = p + s if 2 * m == c else jnp.where(masks[level], s, p)
        m *= 2
        level += 1
    return p, qi, ke, d


def _gla_masks(c):
    t_idx = lax.broadcasted_iota(jnp.int32, (c, c), 0)
    s_idx = lax.broadcasted_iota(jnp.int32, (c, c), 1)
    m = GLA_SUB
    shift = m.bit_length() - 1
    same = (t_idx >> shift) == (s_idx >> shift)
    masks = [jnp.logical_and(same, s_idx <= t_idx), jnp.logical_and(same, s_idx >= t_idx)]
    while 2 * m < c:
        masks.append(((t_idx >> shift) ^ (s_idx >> shift)) == 1)
        m *= 2
        shift += 1
    return masks


def _gla_kernel(q_ref, v_ref, gate_ref, gf_ref, kf_ref, gb_ref, kb_ref, ng_ref, o_ref,
                oi_scr, qi_scr, ke_scr, p_scr, u_scr, d_scr, st_scr, *, n_ctx):
    t = q_ref.shape[0]
    n_chunks = t // GLA_CHUNK
    n_ctx_chunks = n_ctx // GLA_CHUNK
    n_groups = n_chunks // GLA_GROUP
    fwd, bwd = slice(0, HEAD), slice(HEAD, 2 * HEAD)

    def chunk_rows(n):
        start = n * GLA_CHUNK
        return pl.ds(start if isinstance(n, int) else pl.multiple_of(start, GLA_CHUNK), GLA_CHUNK)

    def group_chunks(group):
        return (group * GLA_GROUP + j for j in range(GLA_GROUP))

    masks = _gla_masks(GLA_CHUNK)

    def local(group):
        for n in group_chunks(group):
            rows = chunk_rows(n)
            p, qi, ke, d = _gla_local(q_ref[rows, :].astype(F32), kf_ref[rows, :].astype(F32),
                                      kb_ref[rows, :].astype(F32), gf_ref[rows, :], gb_ref[rows, :], masks)
            p_scr[n] = p.astype(BF16)
            qi_scr[rows, :] = qi
            ke_scr[rows, :] = ke
            d_scr[n] = d

    def local_products(group):
        for n in group_chunks(group):
            rows = chunk_rows(n)
            v = v_ref[rows, :]
            oi_scr[rows, :] = jnp.dot(p_scr[n], v, preferred_element_type=F32)
            u_scr[n] = jnp.dot(v.astype(F32).T.astype(BF16), ke_scr[rows, :], preferred_element_type=F32)

    def skewed(group, carry):
        local_products(group - 1)
        local(group)
        return carry

    local(0)
    lax.fori_loop(1, n_groups, skewed, 0)
    local_products(n_groups - 1)

    def scan(lanes, first_chunk, step, count, st):
        def body(i, st):
            n = first_chunk + step * i
            st_scr[n, :, lanes] = st.astype(BF16)
            return st * d_scr[n, :, lanes] + u_scr[n, :, lanes]
        return lax.fori_loop(0, count, body, st, unroll=4)

    st0 = jnp.zeros((HEAD, HEAD), F32)
    scan(fwd, 0, 1, n_chunks, st0)
    st_b = scan(bwd, n_ctx_chunks - 1, -1, n_ctx_chunks, st0)
    scan(bwd, n_chunks - 1, -1, n_chunks - n_ctx_chunks, st_b)

    def readout(n, carry):
        rows = chunk_rows(n)
        o = oi_scr[rows, :] + lax.dot_general(qi_scr[rows, :], st_scr[n], _NT, preferred_element_type=F32)
        o = o * lax.rsqrt(jnp.mean(o * o, axis=-1, keepdims=True) + EPS) * ng_ref[...]
        o_ref[rows, :] = (o * gate_ref[rows, :].astype(F32)).astype(o_ref.dtype)
        return carry
    lax.fori_loop(0, n_chunks, readout, 0, unroll=True)


def _gla_call(q, v, gate, gf, kf, gb, kb, norm_g, n_ctx):
    b, t, e = q.shape
    assert t % GLA_CHUNK == 0 and n_ctx % GLA_CHUNK == 0 and (t // GLA_CHUNK) % GLA_GROUP == 0
    n_chunks = t // GLA_CHUNK
    spec = pl.BlockSpec((None, t, HEAD), lambda bi, h: (bi, 0, h))
    return pl.pallas_call(
        functools.partial(_gla_kernel, n_ctx=n_ctx),
        grid=(b, e // HEAD),
        in_specs=[spec] * 7 + [pl.BlockSpec((1, HEAD), lambda bi, h: (0, 0))],
        out_specs=spec,
        out_shape=jax.ShapeDtypeStruct((b, t, e), BF16),
        scratch_shapes=[
            pltpu.VMEM((t, HEAD), F32),
            pltpu.VMEM((t, 2 * HEAD), BF16),
            pltpu.VMEM((t, 2 * HEAD), BF16),
            pltpu.VMEM((n_chunks, GLA_CHUNK, GLA_CHUNK), BF16),
            pltpu.VMEM((n_chunks, HEAD, 2 * HEAD), F32),
            pltpu.VMEM((n_chunks, 1, 2 * HEAD), F32),
            pltpu.VMEM((n_chunks, HEAD, 2 * HEAD), BF16),
        ],
        compiler_params=_cparams("arbitrary", "arbitrary"),
        name="hgrn_recurrence",
    )(q, v, gate, gf, kf, gb, kb, norm_g.reshape(1, HEAD))


def _residual(x, y, mod_ref, fin_ref, final_norm):
    out = x + mod_ref[2:3, :] * y
    if final_norm:
        out = out * lax.rsqrt(jnp.mean(out * out, axis=-1, keepdims=True) + EPS) * fin_ref[...]
    return out


def _out_kernel(og_ref, x_ref, modc_ref, modl_ref, w_ref, b_ref, o_ref, *, n_ctx_tiles):
    y = jnp.dot(og_ref[...], w_ref[...], preferred_element_type=F32) + b_ref[...]
    is_ctx = pl.program_id(1) < n_ctx_tiles

    @pl.when(is_ctx)
    def _():
        o_ref[...] = _residual(x_ref[...], y, modc_ref, None, False)

    @pl.when(jnp.logical_not(is_ctx))
    def _():
        o_ref[...] = _residual(x_ref[...], y, modl_ref, None, False)


def _out_call(og, xa, modc, modl, w, layer, bias, n_ctx, skip_ctx):
    b, t, e = og.shape
    d = xa.shape[2]
    n_ctx_tiles = n_ctx // ROWS
    off = n_ctx_tiles if skip_ctx else 0
    n_tiles = t // ROWS - off
    return pl.pallas_call(
        functools.partial(_out_kernel, n_ctx_tiles=0 if skip_ctx else n_ctx_tiles),
        grid=(b, n_tiles),
        in_specs=[
            pl.BlockSpec((None, ROWS, e), lambda bi, i: (bi, i + off, 0)),
            pl.BlockSpec((None, ROWS, d), lambda bi, i: (bi, i + off, 0)),
            pl.BlockSpec((None, 3, d), lambda bi, i: (0, 0, 0)),
            pl.BlockSpec((None, 3, d), lambda bi, i: (bi, 0, 0)),
            pl.BlockSpec((None, e, d), lambda bi, i: (layer, 0, 0)),
            pl.BlockSpec((1, d), lambda bi, i: (0, 0)),
        ],
        out_specs=pl.BlockSpec((None, ROWS, d), lambda bi, i: (bi, i, 0)),
        out_shape=jax.ShapeDtypeStruct((b, n_tiles * ROWS, d), F32),
        compiler_params=_cparams("arbitrary", "arbitrary"),
        name="hgrn_out_proj",
    )(og, xa, modc, modl, w, bias)


def _ln_gate_project(c_scr, gate_slab, lng_ref, lnb_ref, w_ref, n_slabs, rows):
    e = n_slabs * CONV_LANES
    s1 = jnp.zeros((rows, 1), F32)
    for cc in range(n_slabs):
        s1 = s1 + jnp.sum(c_scr[cc], axis=-1, keepdims=True)
    mean = s1 / e
    s2 = jnp.zeros((rows, 1), F32)
    for cc in range(n_slabs):
        dv = c_scr[cc] - mean
        s2 = s2 + jnp.sum(dv * dv, axis=-1, keepdims=True)
    rstd = lax.rsqrt(s2 / e + EPS)
    y = None
    for cc in range(n_slabs):
        lanes = slice(cc * CONV_LANES, (cc + 1) * CONV_LANES)
        u = (c_scr[cc] - mean) * rstd * lng_ref[:, lanes] + lnb_ref[:, lanes]
        u = (_silu(u) * gate_slab(lanes)).astype(BF16)
        part = jnp.dot(u, w_ref[lanes, :], preferred_element_type=F32)
        y = part if y is None else y + part
    return y


def _convh_kernel(u_ref, gate_ref, x_ref, modc_ref, modl_ref, dw_ref, dwb_ref, lng_ref, lnb_ref,
                  w_ref, b_ref, sel_ref, o_ref, pad_scr, shift_scr, c_scr, *, n_ctx_tiles):
    n_slabs = pad_scr.shape[0]
    n_rows = ROWS // GRID_W
    halo = CONV_HALO
    is_ctx = pl.program_id(1) < n_ctx_tiles

    for cc in range(n_slabs):
        lanes = slice(cc * CONV_LANES, (cc + 1) * CONV_LANES)
        for r in range(n_rows):
            pad_scr[cc, r, halo:halo + GRID_W, :] = u_ref[r * GRID_W:(r + 1) * GRID_W, lanes]

    zeros = jnp.zeros((halo, CONV_LANES), BF16)

    @pl.when(is_ctx)
    def _():
        for cc in range(n_slabs):
            lanes = slice(cc * CONV_LANES, (cc + 1) * CONV_LANES)
            for r in range(n_rows):
                lo, hi = r * GRID_W, (r + 1) * GRID_W
                pad_scr[cc, r, 0:halo, :] = zeros if r == 0 else u_ref[lo - halo:lo, lanes]
                pad_scr[cc, r, halo + GRID_W:, :] = zeros if r == n_rows - 1 else u_ref[hi:hi + halo, lanes]

    @pl.when(jnp.logical_not(is_ctx))
    def _():
        for cc in range(n_slabs):
            for r in range(n_rows):
                pad_scr[cc, r, 0:halo, :] = zeros
                pad_scr[cc, r, halo + GRID_W:, :] = zeros

    def conv_slab(cc, carry):
        w = dw_ref[cc]
        for r in range(n_rows):
            acc = jnp.zeros((GRID_W, CONV_LANES), F32)
            span = CONV_SPAN
            shifted = jnp.dot(sel_ref[...], pad_scr[cc, r], preferred_element_type=F32)
            shift_scr[r] = shifted.reshape(SUBLANES, span, CONV_LANES)
            for res in range(SUBLANES):
                for off in range(0, span - GRID_W + 1, SUBLANES):
                    k = off + res - (halo - CONV_PAD)
                    if 0 <= k < CONV_W:
                        acc = acc + shift_scr[r, res, off:off + GRID_W, :] * w[k:k + 1, :]
            c_scr[cc, r * GRID_W:(r + 1) * GRID_W, :] = acc + dwb_ref[cc]
        return carry
    lax.fori_loop(0, n_slabs, conv_slab, 0)

    y = _ln_gate_project(c_scr, lambda lanes: gate_ref[:, lanes].astype(F32), lng_ref, lnb_ref,
                         w_ref, n_slabs, ROWS) + b_ref[...]

    @pl.when(is_ctx)
    def _():
        o_ref[...] = _residual(x_ref[...], y, modc_ref, None, False)

    @pl.when(jnp.logical_not(is_ctx))
    def _():
        o_ref[...] = _residual(x_ref[...], y, modl_ref, None, False)


def _convv_kernel(u_ref, gate_ref, x_ref, modl_ref, dw_ref, dwb_ref, lng_ref, lnb_ref,
                  w_ref, b_ref, fin_ref, o_ref, us_scr, c_scr):
    n_slabs = c_scr.shape[0]
    n_grid_rows = u_ref.shape[0]
    tokens = n_grid_rows * VCOLS

    for cc in range(n_slabs):
        us_scr[cc] = u_ref[:, :, cc * CONV_LANES:(cc + 1) * CONV_LANES].astype(F32)

    def conv_slab(cc, carry):
        w = dw_ref[cc]
        for r in range(n_grid_rows):
            acc = jnp.zeros((VCOLS, CONV_LANES), F32)
            for k in range(CONV_W):
                src = r + k - CONV_PAD
                if 0 <= src < n_grid_rows:
                    acc = acc + us_scr[cc, src] * w[k:k + 1, :]
            c_scr[cc, r * VCOLS:(r + 1) * VCOLS, :] = acc + dwb_ref[cc]
        return carry
    lax.fori_loop(0, n_slabs, conv_slab, 0)

    def gate_slab(lanes):
        return gate_ref[:, :, lanes].astype(F32).reshape(tokens, CONV_LANES)

    y = _ln_gate_project(c_scr, gate_slab, lng_ref, lnb_ref, w_ref, n_slabs, tokens) + b_ref[...]
    d = x_ref.shape[-1]
    out = _residual(x_ref[...].reshape(tokens, d), y, modl_ref, fin_ref, True)
    o_ref[...] = out.reshape(n_grid_rows, VCOLS, d)


def _slab_params(dw, dw_b):
    e = dw.shape[1]
    n_slabs = e // CONV_LANES
    dw_s = dw.reshape(CONV_W, n_slabs, CONV_LANES).transpose(1, 0, 2)
    return dw_s, dw_b.reshape(n_slabs, 1, CONV_LANES), n_slabs


def _convh_call(u, gate, xa, modc, modl, dw, dw_b, ln_g, ln_b, w, layer, bias, n_ctx):
    b, t, e = u.shape
    d = xa.shape[2]
    assert n_ctx in (0, ROWS) and ROWS % GRID_W == 0
    dw_s, dwb_s, n_slabs = _slab_params(dw, dw_b)
    tile = lambda width: pl.BlockSpec((None, ROWS, width), lambda bi, i: (bi, i, 0))
    full2 = lambda a: pl.BlockSpec(a.shape, lambda bi, i: (0, 0))
    full3 = lambda a: pl.BlockSpec(a.shape, lambda bi, i: (0, 0, 0))
    ln_g, ln_b, bias = ln_g.reshape(1, e), ln_b.reshape(1, e), bias.reshape(1, d)
    padded = GRID_W + 2 * CONV_HALO
    picked = (jnp.arange(SUBLANES)[:, None] + jnp.arange(CONV_SPAN)[None, :]).reshape(-1, 1)
    sel = (picked == jnp.arange(padded)[None, :]).astype(BF16)
    return pl.pallas_call(
        functools.partial(_convh_kernel, n_ctx_tiles=n_ctx // ROWS),
        grid=(b, t // ROWS),
        in_specs=[
            tile(e), tile(e), tile(d),
            pl.BlockSpec((None, 3, d), lambda bi, i: (0, 0, 0)),
            pl.BlockSpec((None, 3, d), lambda bi, i: (bi, 0, 0)),
            full3(dw_s), full3(dwb_s), full2(ln_g), full2(ln_b),
            pl.BlockSpec((None, e, d), lambda bi, i: (layer, 0, 0)),
            full2(bias), full2(sel),
        ],
        out_specs=tile(d),
        out_shape=jax.ShapeDtypeStruct((b, t, d), F32),
        scratch_shapes=[
            pltpu.VMEM((n_slabs, ROWS // GRID_W, padded, CONV_LANES), BF16),
            pltpu.VMEM((ROWS // GRID_W, SUBLANES, CONV_SPAN, CONV_LANES), F32),
            pltpu.VMEM((n_slabs, ROWS, CONV_LANES), F32),
        ],
        compiler_params=_cparams("arbitrary", "arbitrary"),
        name="conv_rows_out_proj",
    )(u, gate, xa, modc, modl, dw_s, dwb_s, ln_g, ln_b, w, bias, sel)


def _convv_call(u, gate, x, modl, dw, dw_b, ln_g, ln_b, w, layer, bias, final_g):
    b, l, e = u.shape
    d = x.shape[2]
    n_grid_rows = l // GRID_W
    dw_s, dwb_s, n_slabs = _slab_params(dw, dw_b)
    grid4 = lambda a: a.reshape(b, n_grid_rows, GRID_W, a.shape[-1])
    tile = lambda width: pl.BlockSpec((None, n_grid_rows, VCOLS, width), lambda bi, i: (bi, 0, i, 0))
    full2 = lambda a: pl.BlockSpec(a.shape, lambda bi, i: (0, 0))
    full3 = lambda a: pl.BlockSpec(a.shape, lambda bi, i: (0, 0, 0))
    ln_g, ln_b, bias, final_g = ln_g.reshape(1, e), ln_b.reshape(1, e), bias.reshape(1, d), final_g.reshape(1, d)
    out = pl.pallas_call(
        _convv_kernel,
        grid=(b, GRID_W // VCOLS),
        in_specs=[
            tile(e), tile(e), tile(d),
            pl.BlockSpec((None, 3, d), lambda bi, i: (bi, 0, 0)),
            full3(dw_s), full3(dwb_s), full2(ln_g), full2(ln_b),
            pl.BlockSpec((None, e, d), lambda bi, i: (layer, 0, 0)),
            full2(bias), full2(final_g),
        ],
        out_specs=tile(d),
        out_shape=jax.ShapeDtypeStruct((b, n_grid_rows, GRID_W, d), F32),
        scratch_shapes=[
            pltpu.VMEM((n_slabs, n_grid_rows, VCOLS, CONV_LANES), F32),
            pltpu.VMEM((n_slabs, n_grid_rows * VCOLS, CONV_LANES), F32),
        ],
        compiler_params=_cparams("arbitrary", "arbitrary"),
        name="conv_cols_out_proj_final_norm",
    )(grid4(u), grid4(gate), grid4(x), modl, dw_s, dwb_s, ln_g, ln_b, w, bias, final_g)
    return out.reshape(b, l, d)


def _hgrn_layer(xa, modc, modl, norm_g, w_in, lb, hnorm_g, w_out, j, n_ctx, last_recurrent):
    e = w_out.shape[1]
    d = xa.shape[2]
    q, v, gate, gf, kf, gb, kb = _proj_call(
        "hgrn_proj", xa, modc, modl, norm_g, [(w_in, j, i * e) for i in (0, 1, 4, 2, 3)],
        [lb[0, j].reshape(1, e), lb[1, j].reshape(1, e)],
        lambda accs, aux: _epi_qvgate(accs[:3], aux) + _epi_forget(accs[3:], aux),
        [BF16] * 3 + [F32, BF16, F32, BF16], e, n_ctx, tn=256)
    og = _gla_call(q, v, gate, gf, kf, gb, kb, hnorm_g[j], n_ctx)
    return _out_call(og, xa, modc, modl, w_out, j, jnp.zeros((1, d), F32), n_ctx, skip_ctx=last_recurrent)


def _conv_proj(xa, modc, modl, norm_g, w_in, b_in, j, n_ctx):
    e = w_in.shape[2] // 3
    bias = b_in[j].reshape(1, 3 * e)
    return _proj_call("conv_proj", xa, modc, modl, norm_g,
                      [(w_in, j, 0), (w_in, j, e), (w_in, j, 2 * e)],
                      [bias[:, :e], bias[:, e:2 * e], bias[:, 2 * e:]],
                      _epi_conv, [BF16, BF16], e, n_ctx, tn=256)


def kernel(x, c, ctx, c_ctx, norm_g, ada_w, ada_b, hgrn_w_in, hgrn_lb_logits, hgrn_norm_g, hgrn_w_out,
           conv_w_in, conv_b_in, conv_dw, conv_dw_b, conv_ln_g, conv_ln_b, conv_w_out, conv_b_out,
           final_norm_g):
    b, l, d = x.shape
    n_ctx = ctx.shape[1]
    depth = norm_g.shape[0]
    assert depth == 4, "layer schedule below is written for the 4-layer hybrid"

    bp = -(-(b + 1) // 8) * 8
    cs = jnp.concatenate([c, c_ctx[None, :], jnp.zeros((bp - b - 1, d), F32)], axis=0)
    mod = _ada_call(cs, ada_w, ada_b).reshape(depth, bp, 3, d)
    lb = _lb_call(hgrn_lb_logits)
    xa = jnp.concatenate([ctx, x], axis=1)
    hgrn_w_in, hgrn_w_out = hgrn_w_in.astype(BF16), hgrn_w_out.astype(BF16)
    conv_w_in, conv_w_out = conv_w_in.astype(BF16), conv_w_out.astype(BF16)

    def mods(i):
        return mod[i, b:b + 1], mod[i, :b]

    modc, modl = mods(0)
    xa = _hgrn_layer(xa, modc, modl, norm_g[0], hgrn_w_in, lb, hgrn_norm_g, hgrn_w_out, 0, n_ctx, False)
    modc, modl = mods(1)
    u, gate = _conv_proj(xa, modc, modl, norm_g[1], conv_w_in, conv_b_in, 0, n_ctx)
    xa = _convh_call(u, gate, xa, modc, modl, conv_dw[0], conv_dw_b[0], conv_ln_g[0], conv_ln_b[0],
                     conv_w_out, 0, conv_b_out[0], n_ctx)
    modc, modl = mods(2)
    xl = _hgrn_layer(xa, modc, modl, norm_g[2], hgrn_w_in, lb, hgrn_norm_g, hgrn_w_out, 1, n_ctx, True)
    modc, modl = mods(3)
    u, gate = _conv_proj(xl, modc, modl, norm_g[3], conv_w_in, conv_b_in, 1, 0)
    return _convv_call(u, gate, xl, modl, conv_dw[1], conv_dw_b[1], conv_ln_g[1], conv_ln_b[1],
                       conv_w_out, 1, conv_b_out[1], final_norm_g)
```
